```python
import jax, jax.numpy as jnp
from jax import lax
import numpy as np

D_MODEL = 2048
BATCH = 4
SEQ = 4096
DEPTH = 1
DEC_BATCH = 1
DEC_SEQ = 8192
PAST_LEN = 128

GRID_W = 64
D_POOL = 1024
POOL_WINDOWS = (2, 4, 8, 16)
N_POOL_GROUPS = len(POOL_WINDOWS)
POOL_GROUP = D_POOL // N_POOL_GROUPS
D_NA = D_MODEL - D_POOL
NA_HEADS = 16
NA_HEAD_DIM = D_NA // NA_HEADS
NA_ROWS_MAX = 8
NA_COLS = 16
D_IN = D_POOL + 3 * D_NA
N_MEM = 256
XA_HEADS = 4
XA_HEAD_DIM = D_MODEL // XA_HEADS
N_EXPERTS = 32
TOP_K = 4
D_EXPERT = D_MODEL
SWIGLU_LIMIT = 7.0
SWIGLU_ALPHA = 1.702
MOE_BLOCK = 256
LN_EPS = 1e-5
DEEPNORM_ALPHA = (2 * DEPTH) ** 0.25
DEEPNORM_BETA = (8 * DEPTH) ** -0.25
NEG_INF = -1e30

kernel_name = "hybrid_pool_natten_moe_encoder"


def layer_norm(x, g, b):
    xf = x.astype(jnp.float32)
    mu = jnp.mean(xf, axis=-1, keepdims=True)
    xc = xf - mu
    var = jnp.mean(xc * xc, axis=-1, keepdims=True)
    y = xc * lax.rsqrt(var + LN_EPS) * g.astype(jnp.float32) + b.astype(jnp.float32)
    return y.astype(x.dtype)


def multiscale_pool(u, w_pool, pool_scale):
    B, L, _ = u.shape
    uf = u.astype(jnp.float32)
    cs = jnp.concatenate([jnp.zeros((B, 1, D_POOL), jnp.float32), jnp.cumsum(uf, axis=1)], axis=1)
    t = jnp.arange(L)
    outs = []
    for g, w in enumerate(POOL_WINDOWS):
        lo = jnp.clip(t - w // 2, 0, L)
        hi = jnp.clip(t - w // 2 + w, 0, L)
        sl = slice(g * POOL_GROUP, (g + 1) * POOL_GROUP)
        csg = cs[:, :, sl]
        mean = (csg[:, hi] - csg[:, lo]) / (hi - lo).astype(jnp.float32)[None, :, None]
        outs.append(mean - uf[:, :, sl])
    p = jnp.stack(outs, axis=2).astype(u.dtype)
    y = jnp.einsum('blgc,gcd->blgd', p, w_pool).reshape(B, L, D_POOL)
    return y * pool_scale


def neighbourhood_attention(q, k, v, rpb):
    B, L = q.shape[0], q.shape[1]
    rows = L // GRID_W
    kr = min(NA_ROWS_MAX, rows)
    to_grid = lambda a: a.reshape(B, rows, GRID_W, NA_HEADS, NA_HEAD_DIM)
    qg, kg, vg = to_grid(q), to_grid(k), to_grid(v)
    r = jnp.arange(rows)
    row_start = jnp.clip(r - kr // 2, 0, rows - kr)
    key_rows = row_start[:, None] + jnp.arange(kr)[None, :]
    kb = kg[:, key_rows]
    vb = vg[:, key_rows]
    c = jnp.arange(GRID_W)
    col_start = jnp.clip(c - NA_COLS // 2, 0, GRID_W - NA_COLS)
    col_off = c[None, :] - col_start[:, None]
    col_mask = (col_off >= 0) & (col_off < NA_COLS)
    dr_idx = key_rows - r[:, None] + NA_ROWS_MAX - 1
    dc_idx = jnp.clip(c[None, :] - c[:, None] + NA_COLS - 1, 0, 2 * NA_COLS - 2)
    bias = rpb[:, dr_idx[:, :, None, None], dc_idx[None, None]].astype(jnp.float32)
    bias = jnp.where(col_mask[None, None, None], bias, NEG_INF)
    bias = bias.transpose(0, 1, 3, 2, 4)
    s = jnp.einsum('brqhd,brjkhd->bhrqjk', qg, kb).astype(jnp.float32) * (NA_HEAD_DIM ** -0.5)
    s = s + bias[None]
    p = jax.nn.softmax(s.reshape(B, NA_HEADS, rows, GRID_W, kr * GRID_W), axis=-1)
    p = p.reshape(s.shape).astype(v.dtype)
    o = jnp.einsum('bhrqjk,brjkhd->brqhd', p, vb)
    return o.reshape(B, L, D_NA)


def memory_cross_attention(h, mem, w_xq, w_xkv, w_xo):
    B, L, _ = h.shape
    q = (h @ w_xq).reshape(B, L, XA_HEADS, XA_HEAD_DIM)
    kv = (mem @ w_xkv).reshape(B, mem.shape[1], 2, XA_HEADS, XA_HEAD_DIM)
    k, v = kv[:, :, 0], kv[:, :, 1]
    s = jnp.einsum('blhd,bmhd->bhlm', q, k).astype(jnp.float32) * (XA_HEAD_DIM ** -0.5)
    p = jax.nn.softmax(s, axis=-1).astype(h.dtype)
    o = jnp.einsum('bhlm,bmhd->blhd', p, v).reshape(B, L, D_MODEL)
    return o @ w_xo


def routed_experts(h, w_router, b_router, w_gu, b_gu, w_down, b_down):
    T = h.shape[0]
    TK = T * TOP_K
    logits = (h @ w_router).astype(jnp.float32) + b_router.astype(jnp.float32)
    top_v, top_i = lax.top_k(logits, TOP_K)
    gates = jax.nn.softmax(top_v, axis=-1)
    flat_e = top_i.reshape(TK).astype(jnp.int32)
    flat_g = gates.reshape(TK)
    flat_tok = jnp.arange(TK, dtype=jnp.int32) // TOP_K
    counts = jnp.bincount(flat_e, length=N_EXPERTS)
    padded = (counts + MOE_BLOCK - 1) // MOE_BLOCK * MOE_BLOCK
    padded_end = jnp.cumsum(padded)
    padded_start = padded_end - padded
    group_start = jnp.cumsum(counts) - counts
    order = jnp.argsort(flat_e)
    sorted_e = flat_e[order]
    dest = padded_start[sorted_e] + (jnp.arange(TK) - group_start[sorted_e])
    n_slots = TK + N_EXPERTS * MOE_BLOCK
    n_blocks = n_slots // MOE_BLOCK
    slot_tok = jnp.full((n_slots,), T, jnp.int32).at[dest].set(flat_tok[order])
    slot_gate = jnp.zeros((n_slots,), jnp.float32).at[dest].set(flat_g[order])
    block_expert = jnp.clip(jnp.searchsorted(padded_end, jnp.arange(n_blocks) * MOE_BLOCK, side='right'),
                            0, N_EXPERTS - 1)
    h_pad = jnp.concatenate([h, jnp.zeros((1, D_MODEL), h.dtype)], axis=0)
    xs = h_pad[slot_tok].reshape(n_blocks, MOE_BLOCK, D_MODEL)

    def expert_block(args):
        xb, e = args
        gu = xb @ w_gu[e] + b_gu[e]
        gate = jnp.minimum(gu[:, :D_EXPERT], SWIGLU_LIMIT)
        up = jnp.clip(gu[:, D_EXPERT:], -SWIGLU_LIMIT, SWIGLU_LIMIT)
        act = gate * jax.nn.sigmoid(SWIGLU_ALPHA * gate) * (up + 1)
        return act @ w_down[e] + b_down[e]

    out = lax.map(expert_block, (xs, block_expert)).reshape(n_slots, D_MODEL)
    out = out.astype(jnp.float32) * slot_gate[:, None]
    y = jax.ops.segment_sum(out, slot_tok, num_segments=T + 1)[:T]
    return y.astype(h.dtype)


def encoder_layer(x, mem, w_in, w_pool, pool_scale, rpb, w_out, ln1_g, ln1_b,
                  w_xq, w_xkv, w_xo, ln2_g, ln2_b,
                  w_router, b_router, w_gu, b_gu, w_down, b_down, ln3_g, ln3_b):
    B, L, _ = x.shape
    u = x @ w_in
    a = u[..., :D_POOL]
    qkv = u[..., D_POOL:].reshape(B, L, 3, NA_HEADS, NA_HEAD_DIM)
    ya = multiscale_pool(a, w_pool, pool_scale)
    yb = neighbourhood_attention(qkv[:, :, 0], qkv[:, :, 1], qkv[:, :, 2], rpb)
    mix = jnp.concatenate([ya, yb], axis=-1) @ w_out
    x = layer_norm(DEEPNORM_ALPHA * x + mix, ln1_g, ln1_b)
    x = layer_norm(DEEPNORM_ALPHA * x + memory_cross_attention(x, mem, w_xq, w_xkv, w_xo), ln2_g, ln2_b)
    h = routed_experts(x.reshape(B * L, D_MODEL), w_router, b_router, w_gu, b_gu, w_down, b_down)
    x = layer_norm(DEEPNORM_ALPHA * x + h.reshape(B, L, D_MODEL), ln3_g, ln3_b)
    return x


def setup_inputs(seed: int = 0) -> dict:
    key = jax.random.key(seed)
    ks = jax.random.split(key, 32)
    nrm = lambda k, shape, s: jax.random.normal(k, shape, jnp.float32) * s
    return {
        "x_prompt": nrm(ks[0], (BATCH, SEQ, D_MODEL), 1.0),
        "x_sample": nrm(ks[1], (DEC_BATCH, DEC_SEQ, D_MODEL), 1.0),
        "mem_prompt": nrm(ks[2], (BATCH, N_MEM, D_MODEL), 1.0),
        "mem_sample": nrm(ks[3], (DEC_BATCH, N_MEM, D_MODEL), 1.0),
        "w_in": nrm(ks[4], (DEPTH, D_MODEL, D_IN), D_MODEL ** -0.5),
        "w_pool": nrm(ks[5], (DEPTH, N_POOL_GROUPS, POOL_GROUP, POOL_GROUP), POOL_GROUP ** -0.5),
        "pool_scale": 1.0 + nrm(ks[6], (DEPTH, D_POOL), 0.1),
        "rpb": nrm(ks[7], (DEPTH, NA_HEADS, 2 * NA_ROWS_MAX - 1, 2 * NA_COLS - 1), 0.1),
        "w_out": nrm(ks[8], (DEPTH, D_MODEL, D_MODEL), D_MODEL ** -0.5 * DEEPNORM_BETA),
        "ln1_g": 1.0 + nrm(ks[9], (DEPTH, D_MODEL), 0.05),
        "ln1_b": nrm(ks[10], (DEPTH, D_MODEL), 0.02),
        "w_xq": nrm(ks[11], (DEPTH, D_MODEL, D_MODEL), D_MODEL ** -0.5),
        "w_xkv": nrm(ks[12], (DEPTH, D_MODEL, 2 * D_MODEL), D_MODEL ** -0.5),
        "w_xo": nrm(ks[13], (DEPTH, D_MODEL, D_MODEL), D_MODEL ** -0.5 * DEEPNORM_BETA),
        "ln2_g": 1.0 + nrm(ks[14], (DEPTH, D_MODEL), 0.05),
        "ln2_b": nrm(ks[15], (DEPTH, D_MODEL), 0.02),
        "w_router": nrm(ks[16], (DEPTH, D_MODEL, N_EXPERTS), D_MODEL ** -0.5),
        "b_router": nrm(ks[17], (DEPTH, N_EXPERTS), 0.01),
        "w_gu": nrm(ks[18], (DEPTH, N_EXPERTS, D_MODEL, 2 * D_EXPERT), D_MODEL ** -0.5),
        "b_gu": nrm(ks[19], (DEPTH, N_EXPERTS, 2 * D_EXPERT), 0.02),
        "w_down": nrm(ks[20], (DEPTH, N_EXPERTS, D_EXPERT, D_MODEL), D_EXPERT ** -0.5 * DEEPNORM_BETA),
        "b_down": nrm(ks[21], (DEPTH, N_EXPERTS, D_MODEL), 0.02),
        "ln3_g": 1.0 + nrm(ks[22], (DEPTH, D_MODEL), 0.05),
        "ln3_b": nrm(ks[23], (DEPTH, D_MODEL), 0.02),
    }


def reference(x_prompt, x_sample, mem_prompt, mem_sample, w_in, w_pool, pool_scale, rpb, w_out,
              ln1_g, ln1_b, w_xq, w_xkv, w_xo, ln2_g, ln2_b,
              w_router, b_router, w_gu, b_gu, w_down, b_down, ln3_g, ln3_b):
    def run(x, mem):
        for l in range(DEPTH):
            x = encoder_layer(x, mem, w_in[l], w_pool[l], pool_scale[l], rpb[l], w_out[l],
                              ln1_g[l], ln1_b[l], w_xq[l], w_xkv[l], w_xo[l], ln2_g[l], ln2_b[l],
                              w_router[l], b_router[l], w_gu[l], b_gu[l], w_down[l], b_down[l],
                              ln3_g[l], ln3_b[l])
        return x

    y_prompt = run(x_prompt, mem_prompt)
    y_sample = run(x_sample, mem_sample)
    return (y_prompt, y_sample)
```

```python
import functools

import numpy as np
import jax
import jax.numpy as jnp
from jax import lax
from jax.experimental import pallas as pl
from jax.experimental.pallas import tpu as pltpu

F32 = jnp.float32
BF16 = jnp.bfloat16

D_MODEL = 2048
BATCH, SEQ = 4, 4096
DEC_SEQ = 8192
T_PROMPT = BATCH * SEQ
T_SAMPLE = DEC_SEQ
T_ALL = T_PROMPT + T_SAMPLE
GRID_W = 64
D_POOL = 1024
POOL_WINDOWS = (2, 4, 8, 16)
POOL_GROUP = 256
D_NA = 1024
NA_HEADS = 16
NA_HEAD_DIM = 64
NA_ROWS = 8
NA_COLS = 16
D_IN = D_POOL + 3 * D_NA
N_MEM = 256
XA_HEADS = 4
XA_HEAD_DIM = 512
N_EXPERTS = 32
TOP_K = 4
D_EXPERT = 2048
SWIGLU_LIMIT = 7.0
SWIGLU_ALPHA = 1.702
LN_EPS = 1e-5
DEEPNORM_ALPHA = 2.0 ** 0.25
NEG_INF = -1e30

LANES = 128
VMEM_LIMIT = 56 * 1024 * 1024
TM_IN = 512
TM_SEQ = 512
POOL_HALO = 16
TM_ROW = 256
TM_EXP = 512
TN_GU = 512
TN_DOWN = 1024
TS_GATHER = 1024
TT_COMB = 256
HEADS_PER_GROUP = 2
N_SLOTS = T_ALL * TOP_K + N_EXPERTS * TM_EXP
N_BLOCKS = N_SLOTS // TM_EXP


def _cparams(n_axes=1):
    return pltpu.CompilerParams(dimension_semantics=("arbitrary",) * n_axes,
                                vmem_limit_bytes=VMEM_LIMIT)


def _dot(a, b):
    return jnp.dot(a, b, preferred_element_type=F32)


def _dot_nt(a, b):
    return lax.dot_general(a, b, (((1,), (1,)), ((), ())), preferred_element_type=F32)


def _layer_norm(x, g, b):
    mu = jnp.mean(x, axis=-1, keepdims=True)
    xc = x - mu
    var = jnp.mean(xc * xc, axis=-1, keepdims=True)
    return xc * lax.rsqrt(var + LN_EPS) * g + b


def _const_spec(shape):
    nd = len(shape)
    return pl.BlockSpec(shape, lambda *_: (0,) * nd, pipeline_mode=pl.Buffered(1))


def _inproj_kernel(xp_ref, xs_ref, w_ref, o_ref, *, n_prompt_tiles):
    i = pl.program_id(0)
    x = jnp.where(i < n_prompt_tiles, xp_ref[...], xs_ref[...]).astype(BF16)
    for c in range(D_IN // 1024):
        sl = slice(c * 1024, (c + 1) * 1024)
        o_ref[:, sl] = _dot(x, w_ref[:, sl]).astype(BF16)


def _two_group_specs(tm, n_prompt_tiles):
    last = n_prompt_tiles - 1
    return (pl.BlockSpec((tm, D_MODEL), lambda i: (jnp.minimum(i, last), 0)),
            pl.BlockSpec((tm, D_MODEL), lambda i: (jnp.maximum(i - n_prompt_tiles, 0), 0)))


def _inproj(xp, xs, w_in):
    npt = T_PROMPT // TM_IN
    spec_p, spec_s = _two_group_specs(TM_IN, npt)
    return pl.pallas_call(
        functools.partial(_inproj_kernel, n_prompt_tiles=npt),
        grid=(T_ALL // TM_IN,),
        in_specs=[spec_p, spec_s, _const_spec((D_MODEL, D_IN))],
        out_specs=pl.BlockSpec((TM_IN, D_IN), lambda i: (i, 0)),
        out_shape=jax.ShapeDtypeStruct((T_ALL, D_IN), BF16),
        compiler_params=_cparams(), name="inproj")(xp, xs, w_in)


def _seq_position(i):
    per_prompt = SEQ // TM_SEQ
    n_prompt = T_PROMPT // TM_SEQ
    is_prompt = i < n_prompt
    pos = jnp.where(is_prompt, lax.rem(i, per_prompt), i - n_prompt)
    nblk = jnp.where(is_prompt, per_prompt, DEC_SEQ // TM_SEQ)
    return pos, nblk


def _pool_kernel(u_ref, prev_ref, next_ref, a_ref, wp_ref, sc_ref, o_ref):
    pos, nblk = _seq_position(pl.program_id(0))
    cur = u_ref[...]
    zero = jnp.zeros((POOL_HALO, D_POOL), BF16)
    prev = jnp.where(pos == 0, zero, prev_ref[...])
    nxt = jnp.where(pos == nblk - 1, zero, next_ref[...])
    ext = jnp.concatenate([prev, cur, nxt], axis=0)
    t = pos * TM_SEQ + lax.broadcasted_iota(jnp.int32, (TM_SEQ, 1), 0)
    seq_len = nblk * TM_SEQ
    for g, w in enumerate(POOL_WINDOWS):
        sl = slice(g * POOL_GROUP, (g + 1) * POOL_GROUP)
        win_sum = _dot(a_ref[g], ext[:, sl])
        cnt = (jnp.minimum(t - w // 2 + w, seq_len) - jnp.maximum(t - w // 2, 0)).astype(F32)
        p = win_sum / cnt - cur[:, sl].astype(F32)
        y = _dot(p.astype(BF16), wp_ref[g]) * sc_ref[:, sl]
        o_ref[:, sl] = y.astype(BF16)


def _pool_band_matrices():
    t = np.arange(TM_SEQ)[:, None]
    c = np.arange(TM_SEQ + 2 * POOL_HALO)[None, :] - POOL_HALO
    mats = [((c - t >= -(w // 2)) & (c - t <= w // 2 - 1)) for w in POOL_WINDOWS]
    return jnp.asarray(np.stack(mats).astype(np.float32), dtype=BF16)


def _pool_mixer(u, w_pool, pool_scale):
    per_halo = TM_SEQ // POOL_HALO
    n_halo = T_ALL // POOL_HALO
    return pl.pallas_call(
        _pool_kernel,
        grid=(T_ALL // TM_SEQ,),
        in_specs=[
            pl.BlockSpec((TM_SEQ, D_POOL), lambda i: (i, 0)),
            pl.BlockSpec((POOL_HALO, D_POOL), lambda i: (jnp.maximum(i * per_halo - 1, 0), 0)),
            pl.BlockSpec((POOL_HALO, D_POOL), lambda i: (jnp.minimum((i + 1) * per_halo, n_halo - 1), 0)),
            _const_spec((len(POOL_WINDOWS), TM_SEQ, TM_SEQ + 2 * POOL_HALO)),
            _const_spec((len(POOL_WINDOWS), POOL_GROUP, POOL_GROUP)),
            _const_spec((1, D_POOL)),
        ],
        out_specs=pl.BlockSpec((TM_SEQ, D_POOL), lambda i: (i, 0)),
        out_shape=jax.ShapeDtypeStruct((T_ALL, D_POOL), BF16),
        compiler_params=_cparams(), name="pool_mixer")(
            u, u, u, _pool_band_matrices(), w_pool, pool_scale)


ROWS_PER_STEP = TM_SEQ // GRID_W
KV_WINDOW_ROWS = 2 * ROWS_PER_STEP


def _na_step_geometry(s):
    steps_per_prompt = SEQ // TM_SEQ
    n_prompt_steps = T_PROMPT // TM_SEQ
    is_prompt = s < n_prompt_steps
    jj = jnp.where(is_prompt, lax.rem(s, steps_per_prompt), s - n_prompt_steps)
    rows = jnp.where(is_prompt, SEQ // GRID_W, DEC_SEQ // GRID_W)
    seq_row0 = jnp.where(is_prompt, (s // steps_per_prompt) * (SEQ // GRID_W), T_PROMPT // GRID_W)
    return jj * ROWS_PER_STEP, rows, seq_row0


def _na_window_row(s):
    r0, rows, seq_row0 = _na_step_geometry(s)
    return jnp.clip(r0 - NA_ROWS // 2, 0, rows - KV_WINDOW_ROWS), seq_row0


def _na_kernel(q_ref, k_ref, v_ref, bias_ref, o_ref):
    s = pl.program_id(0)
    r0, rows, _ = _na_step_geometry(s)
    win0, _ = _na_window_row(s)
    gl = HEADS_PER_GROUP * NA_HEAD_DIM
    lane = lax.broadcasted_iota(jnp.int32, (GRID_W, gl), 1)
    n_keys = NA_ROWS * GRID_W

    def row_body(i, carry):
        r = r0 + i
        row_start = jnp.clip(r - NA_ROWS // 2, 0, rows - NA_ROWS)
        k0 = pl.multiple_of((row_start - win0) * GRID_W, GRID_W)
        d0 = NA_ROWS - 1 + row_start - r
        q0 = pl.multiple_of(i * GRID_W, GRID_W)
        for p in range(NA_HEADS // HEADS_PER_GROUP):
            ls = slice(p * gl, (p + 1) * gl)
            q2 = q_ref[pl.ds(q0, GRID_W), ls] * jnp.asarray(NA_HEAD_DIM ** -0.5, BF16)
            zq = jnp.zeros_like(q2)
            qs = jnp.concatenate(
                [jnp.where((lane >= h * NA_HEAD_DIM) & (lane < (h + 1) * NA_HEAD_DIM), q2, zq)
                 for h in range(HEADS_PER_GROUP)], axis=0)
            kw = k_ref[pl.ds(k0, n_keys), ls]
            vw = v_ref[pl.ds(k0, n_keys), ls]
            sc = _dot_nt(qs, kw)
            sc = jnp.concatenate(
                [sc[:, j * LANES:(j + 1) * LANES] + bias_ref[p, d0 + 2 * j] for j in range(n_keys // LANES)],
                axis=1)
            m = jnp.max(sc, axis=-1, keepdims=True)
            e = jnp.exp(sc - m)
            den = jnp.sum(e, axis=-1, keepdims=True)
            o = _dot(e.astype(BF16), vw) / den
            out = o[0:GRID_W]
            for h in range(1, HEADS_PER_GROUP):
                out = jnp.where(lane >= h * NA_HEAD_DIM, o[h * GRID_W:(h + 1) * GRID_W], out)
            o_ref[pl.ds(q0, GRID_W), ls] = out.astype(BF16)
        return carry

    lax.fori_loop(0, ROWS_PER_STEP, row_body, 0)


def _na_bias_table(rpb):
    qc = np.arange(GRID_W)[:, None]
    kc = np.arange(GRID_W)[None, :]
    dc = np.clip(kc - qc + NA_COLS - 1, 0, 2 * NA_COLS - 2)
    col_start = np.clip(qc - NA_COLS // 2, 0, GRID_W - NA_COLS)
    mask = (kc - col_start >= 0) & (kc - col_start < NA_COLS)
    tab = jnp.where(mask[None, None], rpb[:, :, dc].astype(F32), NEG_INF)
    n_dr = 2 * NA_ROWS - 1
    ng = NA_HEADS // HEADS_PER_GROUP
    tab = tab.reshape(ng, HEADS_PER_GROUP, n_dr, GRID_W, GRID_W).transpose(0, 2, 1, 3, 4)
    tab = tab.reshape(ng, n_dr, HEADS_PER_GROUP * GRID_W, GRID_W)
    return jnp.concatenate([tab[:, :-1], tab[:, 1:]], axis=-1)


def _neighbourhood_attention(u, rpb):
    def kv_spec(col_block):
        def index(s):
            win0, seq_row0 = _na_window_row(s)
            return ((seq_row0 + win0) * GRID_W, col_block * D_NA)
        return pl.BlockSpec((pl.Element(KV_WINDOW_ROWS * GRID_W), pl.Element(D_NA)), index)

    bias = _na_bias_table(rpb)
    return pl.pallas_call(
        _na_kernel,
        grid=(T_ALL // TM_SEQ,),
        in_specs=[pl.BlockSpec((TM_SEQ, D_NA), lambda s: (s, 1)),
                  kv_spec(2), kv_spec(3), _const_spec(bias.shape)],
        out_specs=pl.BlockSpec((TM_SEQ, D_NA), lambda s: (s, 0)),
        out_shape=jax.ShapeDtypeStruct((T_ALL, D_NA), BF16),
        compiler_params=_cparams(), name="neighbourhood_attention")(u, u, u, bias)


def _mix_ln_kernel(ya_ref, yb_ref, xp_ref, xs_ref, w_ref, g_ref, b_ref, o_ref, *, n_prompt_tiles):
    i = pl.program_id(0)
    x = jnp.where(i < n_prompt_tiles, xp_ref[...], xs_ref[...])
    mix = _dot(ya_ref[...], w_ref[:D_POOL, :]) + _dot(yb_ref[...], w_ref[D_POOL:, :])
    o_ref[...] = _layer_norm(DEEPNORM_ALPHA * x + mix, g_ref[...], b_ref[...])


def _mix_ln(ya, yb, xp, xs, w_out, g, b):
    npt = T_PROMPT // TM_ROW
    spec_p, spec_s = _two_group_specs(TM_ROW, npt)
    return pl.pallas_call(
        functools.partial(_mix_ln_kernel, n_prompt_tiles=npt),
        grid=(T_ALL // TM_ROW,),
        in_specs=[pl.BlockSpec((TM_ROW, D_POOL), lambda i: (i, 0)),
                  pl.BlockSpec((TM_ROW, D_NA), lambda i: (i, 0)),
                  spec_p, spec_s,
                  _const_spec((D_MODEL, D_MODEL)), _const_spec((1, D_MODEL)), _const_spec((1, D_MODEL))],
        out_specs=pl.BlockSpec((TM_ROW, D_MODEL), lambda i: (i, 0)),
        out_shape=jax.ShapeDtypeStruct((T_ALL, D_MODEL), F32),
        compiler_params=_cparams(), name="mix_ln1")(ya, yb, xp, xs, w_out, g, b)


def _kv_kernel(m_ref, w_ref, o_ref):
    o_ref[...] = _dot(m_ref[...], w_ref[...]).astype(BF16)


def _memory_kv(mem, w_xkv):
    n_mem_rows = mem.shape[0]
    tn = 1024
    return pl.pallas_call(
        _kv_kernel,
        grid=(2 * D_MODEL // tn,),
        in_specs=[_const_spec((n_mem_rows, D_MODEL)), pl.BlockSpec((D_MODEL, tn), lambda j: (0, j))],
        out_specs=pl.BlockSpec((n_mem_rows, tn), lambda j: (0, j)),
        out_shape=jax.ShapeDtypeStruct((n_mem_rows, 2 * D_MODEL), BF16),
        compiler_params=_cparams(), name="memory_kv")(mem, w_xkv)


def _xattn_kernel(x_ref, wq_ref, k_ref, v_ref, o_ref):
    q = _dot(x_ref[...].astype(BF16), wq_ref[...]).astype(BF16)
    for h in range(XA_HEADS):
        sl = slice(h * XA_HEAD_DIM, (h + 1) * XA_HEAD_DIM)
        sc = _dot_nt(q[:, sl], k_ref[:, sl]) * (XA_HEAD_DIM ** -0.5)
        m = jnp.max(sc, axis=-1, keepdims=True)
        e = jnp.exp(sc - m)
        den = jnp.sum(e, axis=-1, keepdims=True)
        o_ref[:, sl] = (_dot(e.astype(BF16), v_ref[:, sl]) / den).astype(BF16)


def _mem_batch(i):
    return jnp.minimum((i * TM_ROW) // SEQ, BATCH)


def _cross_attention(x1, w_xq, kv):
    return pl.pallas_call(
        _xattn_kernel,
        grid=(T_ALL // TM_ROW,),
        in_specs=[pl.BlockSpec((TM_ROW, D_MODEL), lambda i: (i, 0)),
                  _const_spec((D_MODEL, D_MODEL)),
                  pl.BlockSpec((N_MEM, D_MODEL), lambda i: (_mem_batch(i), 0)),
                  pl.BlockSpec((N_MEM, D_MODEL), lambda i: (_mem_batch(i), 1))],
        out_specs=pl.BlockSpec((TM_ROW, D_MODEL), lambda i: (i, 0)),
        out_shape=jax.ShapeDtypeStruct((T_ALL, D_MODEL), BF16),
        compiler_params=_cparams(), name="cross_attention")(x1, w_xq, kv, kv)


def _route_kernel(a_ref, x1_ref, wo_ref, g_ref, b_ref, wr_ref, br_ref,
                  x2_ref, x2p_ref, idx_ref, gate_ref, rank_ref, cnt_ref, carry_ref):
    @pl.when(pl.program_id(0) == 0)
    def _():
        carry_ref[...] = jnp.zeros_like(carry_ref)

    x2 = _layer_norm(DEEPNORM_ALPHA * x1_ref[...] + _dot(a_ref[...], wo_ref[...]), g_ref[...], b_ref[...])
    x2_ref[...] = x2
    half = D_MODEL // 2
    x2p_ref[...] = pltpu.pack_elementwise([x2[:, :half], x2[:, half:]], packed_dtype=BF16)

    lane = lax.broadcasted_iota(jnp.int32, (TM_ROW, LANES), 1)
    lanef = lane.astype(F32)
    logits = jnp.where(lane < N_EXPERTS, _dot(x2.astype(BF16), wr_ref[...]) + br_ref[...], -jnp.inf)
    top_v, top_i, hots = [], [], []
    for _ in range(TOP_K):
        m = jnp.max(logits, axis=-1, keepdims=True)
        first = jnp.min(jnp.where(logits == m, lanef, float(LANES)), axis=-1, keepdims=True)
        hot = lanef == first
        top_v.append(m)
        top_i.append(first)
        hots.append(hot)
        logits = jnp.where(hot, -jnp.inf, logits)
    ex = [jnp.exp(v - top_v[0]) for v in top_v]
    den = ex[0] + ex[1] + ex[2] + ex[3]

    chosen = jnp.zeros((TM_ROW, LANES), F32)
    for hot in hots:
        chosen = jnp.where(hot, 1.0, chosen)
    r_i = lax.broadcasted_iota(jnp.int32, (TM_ROW, TM_ROW), 0)
    c_i = lax.broadcasted_iota(jnp.int32, (TM_ROW, TM_ROW), 1)
    before = jnp.where(c_i < r_i, 1.0, 0.0).astype(BF16)
    base = carry_ref[...] + _dot(before, chosen.astype(BF16))

    idx_out = jnp.zeros((TM_ROW, LANES), jnp.int32)
    gate_out = jnp.zeros((TM_ROW, LANES), F32)
    rank_out = jnp.zeros((TM_ROW, LANES), jnp.int32)
    for k in range(TOP_K):
        rank_k = jnp.sum(jnp.where(hots[k], base, 0.0), axis=-1, keepdims=True)
        idx_out = jnp.where(lane == k, top_i[k].astype(jnp.int32), idx_out)
        gate_out = jnp.where(lane == k, ex[k] / den, gate_out)
        rank_out = jnp.where(lane == k, rank_k.astype(jnp.int32), rank_out)
    idx_ref[...] = idx_out
    gate_ref[...] = gate_out
    rank_ref[...] = rank_out
    carry_ref[...] = carry_ref[...] + jnp.sum(chosen, axis=0, keepdims=True)
    cnt_ref[...] = carry_ref[...]


def _project_norm_route(attn, x1, w_xo, g, b, w_router, b_router):
    row = lambda w: pl.BlockSpec((TM_ROW, w), lambda i: (i, 0))
    return pl.pallas_call(
        _route_kernel,
        grid=(T_ALL // TM_ROW,),
        in_specs=[row(D_MODEL), row(D_MODEL), _const_spec((D_MODEL, D_MODEL)),
                  _const_spec((1, D_MODEL)), _const_spec((1, D_MODEL)),
                  _const_spec((D_MODEL, LANES)), _const_spec((1, LANES))],
        out_specs=[row(D_MODEL), row(D_MODEL // 2), row(LANES), row(LANES), row(LANES),
                   pl.BlockSpec((1, LANES), lambda i: (0, 0))],
        out_shape=[jax.ShapeDtypeStruct((T_ALL, D_MODEL), F32),
                   jax.ShapeDtypeStruct((T_ALL, D_MODEL // 2), jnp.uint32),
                   jax.ShapeDtypeStruct((T_ALL, LANES), jnp.int32),
                   jax.ShapeDtypeStruct((T_ALL, LANES), F32),
                   jax.ShapeDtypeStruct((T_ALL, LANES), jnp.int32),
                   jax.ShapeDtypeStruct((1, LANES), F32)],
        scratch_shapes=[pltpu.VMEM((1, LANES), F32)],
        compiler_params=_cparams(), name="proj_ln2_route")(attn, x1, w_xo, g, b, w_router, b_router)


def _gather_kernel(tok_ref, x2p_ref, o_ref, sem):
    def issue(j, carry):
        pltpu.make_async_copy(x2p_ref.at[pl.ds(tok_ref[j], 1)], o_ref.at[pl.ds(j, 1)], sem).start()
        return carry

    lax.fori_loop(0, TS_GATHER, issue, 0)
    pltpu.make_async_copy(x2p_ref.at[pl.ds(0, TS_GATHER)], o_ref, sem).wait()


def _gather_rows(slot_tok, x2p):
    return pl.pallas_call(
        _gather_kernel,
        grid=(N_SLOTS // TS_GATHER,),
        in_specs=[pl.BlockSpec((TS_GATHER,), lambda i: (i,), memory_space=pltpu.SMEM),
                  pl.BlockSpec(memory_space=pl.ANY)],
        out_specs=pl.BlockSpec((TS_GATHER, D_MODEL // 2), lambda i: (i, 0)),
        out_shape=jax.ShapeDtypeStruct((N_SLOTS, D_MODEL // 2), jnp.uint32),
        scratch_shapes=[pltpu.SemaphoreType.DMA],
        compiler_params=_cparams(), name="gather_rows")(slot_tok, x2p)


def _unpack_rows(xp):
    lo = pltpu.unpack_elementwise(xp, index=0, packed_dtype=BF16, unpacked_dtype=F32)
    hi = pltpu.unpack_elementwise(xp, index=1, packed_dtype=BF16, unpacked_dtype=F32)
    return lo.astype(BF16), hi.astype(BF16)


def _gate_up_kernel(blk_ref, exp_ref, col_ref, valid_ref, x_ref, wg_ref, wu_ref, bg_ref, bu_ref, o_ref):
    @pl.when(valid_ref[pl.program_id(0)] == 1)
    def _():
        half = D_MODEL // 2
        x_lo, x_hi = _unpack_rows(x_ref[...])
        wg = wg_ref[...].astype(BF16)
        wu = wu_ref[...].astype(BF16)
        gate = _dot(x_lo, wg[:half]) + _dot(x_hi, wg[half:]) + bg_ref[...]
        up = _dot(x_lo, wu[:half]) + _dot(x_hi, wu[half:]) + bu_ref[...]
        gate = jnp.minimum(gate, SWIGLU_LIMIT)
        up = jnp.clip(up, -SWIGLU_LIMIT, SWIGLU_LIMIT)
        o_ref[...] = (gate * jax.nn.sigmoid(SWIGLU_ALPHA * gate) * (up + 1.0)).astype(BF16)

    @pl.when(valid_ref[pl.program_id(0)] == 0)
    def _():
        o_ref[...] = jnp.zeros_like(o_ref)


def _down_kernel(blk_ref, exp_ref, col_ref, valid_ref, a_ref, w_ref, b_ref, o_ref):
    @pl.when(valid_ref[pl.program_id(0)] == 1)
    def _():
        o_ref[...] = _dot(a_ref[...], w_ref[...].astype(BF16)) + b_ref[...]

    @pl.when(valid_ref[pl.program_id(0)] == 0)
    def _():
        o_ref[...] = jnp.zeros_like(o_ref)


def _expert_schedule(blocks_per_expert, n_col_tiles):
    n_steps = N_BLOCKS * n_col_tiles
    blk_end = jnp.cumsum(blocks_per_expert)
    blk_start = blk_end - blocks_per_expert
    total = blk_end[-1] * n_col_tiles
    step = jnp.arange(n_steps, dtype=jnp.int32)
    valid = step < total
    s = jnp.minimum(step, total - 1)
    e = jnp.searchsorted(blk_end * n_col_tiles, s, side="right").astype(jnp.int32)
    local = s - blk_start[e] * n_col_tiles
    nb = blocks_per_expert[e]
    col = local // nb
    blk = blk_start[e] + local - col * nb
    spare = step - total
    blk = jnp.where(valid, blk, blk_end[-1] + spare // n_col_tiles)
    col = jnp.where(valid, col, spare % n_col_tiles)
    return blk.astype(jnp.int32), e, col.astype(jnp.int32), valid.astype(jnp.int32)


def _experts(xs, blocks_per_expert, w_gu, b_gu, w_down, b_down):
    n_gu = D_EXPERT // TN_GU
    sched = _expert_schedule(blocks_per_expert, n_gu)
    act = pl.pallas_call(
        _gate_up_kernel,
        grid_spec=pltpu.PrefetchScalarGridSpec(
            num_scalar_prefetch=4, grid=(N_BLOCKS * n_gu,),
            in_specs=[
                pl.BlockSpec((TM_EXP, D_MODEL // 2), lambda s, blk, ex, col, ok: (blk[s], 0)),
                pl.BlockSpec((None, D_MODEL, TN_GU), lambda s, blk, ex, col, ok: (ex[s], 0, col[s])),
                pl.BlockSpec((None, D_MODEL, TN_GU), lambda s, blk, ex, col, ok: (ex[s], 0, n_gu + col[s])),
                pl.BlockSpec((None, 1, TN_GU), lambda s, blk, ex, col, ok: (ex[s], 0, col[s])),
                pl.BlockSpec((None, 1, TN_GU), lambda s, blk, ex, col, ok: (ex[s], 0, n_gu + col[s])),
            ],
            out_specs=pl.BlockSpec((TM_EXP, TN_GU), lambda s, blk, ex, col, ok: (blk[s], col[s]))),
        out_shape=jax.ShapeDtypeStruct((N_SLOTS, D_EXPERT), BF16),
        compiler_params=_cparams(), name="expert_gate_up")(*sched, xs, w_gu, w_gu, b_gu, b_gu)

    n_dn = D_MODEL // TN_DOWN
    sched = _expert_schedule(blocks_per_expert, n_dn)
    return pl.pallas_call(
        _down_kernel,
        grid_spec=pltpu.PrefetchScalarGridSpec(
            num_scalar_prefetch=4, grid=(N_BLOCKS * n_dn,),
            in_specs=[
                pl.BlockSpec((TM_EXP, D_EXPERT), lambda s, blk, ex, col, ok: (blk[s], 0)),
                pl.BlockSpec((None, D_EXPERT, TN_DOWN), lambda s, blk, ex, col, ok: (ex[s], 0, col[s])),
                pl.BlockSpec((None, 1, TN_DOWN), lambda s, blk, ex, col, ok: (ex[s], 0, col[s])),
            ],
            out_specs=pl.BlockSpec((TM_EXP, TN_DOWN), lambda s, blk, ex, col, ok: (blk[s], col[s]))),
        out_shape=jax.ShapeDtypeStruct((N_SLOTS, D_MODEL), F32),
        compiler_params=_cparams(), name="expert_down")(*sched, act, w_down, b_down)


def _combine_kernel(dest_ref, gate_ref, x2_ref, y_ref, g_ref, b_ref, o_ref, buf_ref, sem):
    n = TT_COMB * TOP_K

    def issue(j, carry):
        pltpu.make_async_copy(y_ref.at[pl.ds(dest_ref[j], 1)],
                              buf_ref.at[lax.rem(j, TOP_K), pl.ds(j // TOP_K, 1)], sem).start()
        return carry

    lax.fori_loop(0, n, issue, 0)
    for k in range(TOP_K):
        pltpu.make_async_copy(y_ref.at[pl.ds(0, TT_COMB)], buf_ref.at[k], sem).wait()
    gates = gate_ref[...]
    h = gates[:, 0:1] * buf_ref[0]
    for k in range(1, TOP_K):
        h = h + gates[:, k:k + 1] * buf_ref[k]
    o_ref[...] = _layer_norm(DEEPNORM_ALPHA * x2_ref[...] + h, g_ref[...], b_ref[...])


def _combine(dest_flat, gates, x2, y_slots, g, b, first_tile, n_tiles):
    n = TT_COMB * TOP_K
    return pl.pallas_call(
        _combine_kernel,
        grid=(n_tiles,),
        in_specs=[pl.BlockSpec((n,), lambda i: (i + first_tile,), memory_space=pltpu.SMEM),
                  pl.BlockSpec((TT_COMB, LANES), lambda i: (i + first_tile, 0)),
                  pl.BlockSpec((TT_COMB, D_MODEL), lambda i: (i + first_tile, 0)),
                  pl.BlockSpec(memory_space=pl.ANY),
                  _const_spec((1, D_MODEL)), _const_spec((1, D_MODEL))],
        out_specs=pl.BlockSpec((TT_COMB, D_MODEL), lambda i: (i, 0)),
        out_shape=jax.ShapeDtypeStruct((n_tiles * TT_COMB, D_MODEL), F32),
        scratch_shapes=[pltpu.VMEM((TOP_K, TT_COMB, D_MODEL), F32), pltpu.SemaphoreType.DMA],
        compiler_params=_cparams(), name="combine_ln3")(dest_flat, gates, x2, y_slots, g, b)


def kernel(x_prompt, x_sample, mem_prompt, mem_sample, w_in, w_pool, pool_scale, rpb, w_out,
           ln1_g, ln1_b, w_xq, w_xkv, w_xo, ln2_g, ln2_b,
           w_router, b_router, w_gu, b_gu, w_down, b_down, ln3_g, ln3_b):
    assert w_in.shape[0] == 1, "single-layer problem"
    xp = x_prompt.reshape(T_PROMPT, D_MODEL)
    xs = x_sample.reshape(T_SAMPLE, D_MODEL)
    mem = jnp.concatenate([mem_prompt.reshape(BATCH * N_MEM, D_MODEL),
                           mem_sample.reshape(N_MEM, D_MODEL)], axis=0).astype(BF16)
    row = lambda v: v.reshape(1, -1).astype(F32)

    u = _inproj(xp, xs, w_in[0].astype(BF16))
    ya = _pool_mixer(u, w_pool[0].astype(BF16), row(pool_scale[0]))
    yb = _neighbourhood_attention(u, rpb[0])
    x1 = _mix_ln(ya, yb, xp, xs, w_out[0].astype(BF16), row(ln1_g[0]), row(ln1_b[0]))

    kv = _memory_kv(mem, w_xkv[0].astype(BF16))
    attn = _cross_attention(x1, w_xq[0].astype(BF16), kv)
    w_r = jnp.pad(w_router[0], ((0, 0), (0, LANES - N_EXPERTS))).astype(BF16)
    b_r = jnp.pad(b_router[0].astype(F32), (0, LANES - N_EXPERTS)).reshape(1, LANES)
    x2, x2p, top_i, gates, rank, counts = _project_norm_route(
        attn, x1, w_xo[0].astype(BF16), row(ln2_g[0]), row(ln2_b[0]), w_r, b_r)

    counts = counts[0, :N_EXPERTS].astype(jnp.int32)
    blocks_per_expert = (counts + TM_EXP - 1) // TM_EXP
    blk_end = jnp.cumsum(blocks_per_expert)
    slot_start = (blk_end - blocks_per_expert) * TM_EXP
    dest = (slot_start[top_i[:, :TOP_K]] + rank[:, :TOP_K]).reshape(T_ALL * TOP_K)
    slot_tok = jnp.zeros((N_SLOTS,), jnp.int32).at[dest].set(
        jnp.arange(T_ALL * TOP_K, dtype=jnp.int32) // TOP_K, unique_indices=True)

    xs_slots = _gather_rows(slot_tok, x2p)
    y_slots = _experts(xs_slots, blocks_per_expert, w_gu[0], b_gu[0].reshape(N_EXPERTS, 1, -1),
                       w_down[0], b_down[0].reshape(N_EXPERTS, 1, -1))

    g3, b3 = row(ln3_g[0]), row(ln3_b[0])
    n_p = T_PROMPT // TT_COMB
    y_prompt = _combine(dest, gates, x2, y_slots, g3, b3, 0, n_p)
    y_sample = _combine(dest, gates, x2, y_slots, g3, b3, n_p, T_SAMPLE // TT_COMB)
    return (y_prompt.reshape(BATCH, SEQ, D_MODEL), y_sample.reshape(1, DEC_SEQ, D_MODEL))
```

```python
import functools

import numpy as np
import jax
import jax.numpy as jnp
from jax import lax
from jax.experimental import pallas as pl
from jax.experimental.pallas import tpu as pltpu

F32 = jnp.float32
BF16 = jnp.bfloat16

D_MODEL = 2048
BATCH, SEQ = 4, 4096
DEC_SEQ = 8192
T_PROMPT = BATCH * SEQ
T_SAMPLE = DEC_SEQ
T_ALL = T_PROMPT + T_SAMPLE
GRID_W = 64
D_POOL = 1024
POOL_WINDOWS = (2, 4, 8, 16)
POOL_GROUP = 256
D_NA = 1024
NA_HEADS = 16
NA_HEAD_DIM = 64
NA_ROWS = 8
NA_COLS = 16
D_IN = D_POOL + 3 * D_NA
N_MEM = 256
XA_HEADS = 4
XA_HEAD_DIM = 512
N_EXPERTS = 32
TOP_K = 4
D_EXPERT = 2048
SWIGLU_LIMIT = 7.0
SWIGLU_ALPHA = 1.702
LN_EPS = 1e-5
DEEPNORM_ALPHA = 2.0 ** 0.25
NEG_INF = -1e30

LANES = 128
SUBLANES = 8
VMEM_LIMIT = 56 * 1024 * 1024
TM_IN = 512
TM_SEQ = 512
POOL_HALO = 16
TM_ROW = 256
TM_EXP = 512
TN_GU = 512
TN_DOWN = 1024
TS_GATHER = 1024
TT_COMB = 256
HEADS_PER_GROUP = 2
N_SLOTS = T_ALL * TOP_K + N_EXPERTS * TM_EXP
N_BLOCKS = N_SLOTS // TM_EXP


def _cparams(n_axes=1):
    return pltpu.CompilerParams(dimension_semantics=("arbitrary",) * n_axes,
                                vmem_limit_bytes=VMEM_LIMIT)


def _dot(a, b):
    return jnp.dot(a, b, preferred_element_type=F32)


def _dot_nt(a, b):
    return lax.dot_general(a, b, (((1,), (1,)), ((), ())), preferred_element_type=F32)


def _layer_norm(x, g, b):
    mu = jnp.mean(x, axis=-1, keepdims=True)
    xc = x - mu
    var = jnp.mean(xc * xc, axis=-1, keepdims=True)
    return xc * lax.rsqrt(var + LN_EPS) * g + b


def _const_spec(shape):
    nd = len(shape)
    return pl.BlockSpec(shape, lambda *_: (0,) * nd, pipeline_mode=pl.Buffered(1))


def _inproj_kernel(xp_ref, xs_ref, w_ref, o_ref, *, n_prompt_tiles):
    i = pl.program_id(0)
    x = jnp.where(i < n_prompt_tiles, xp_ref[...], xs_ref[...]).astype(BF16)
    for c in range(D_IN // 1024):
        sl = slice(c * 1024, (c + 1) * 1024)
        o_ref[:, sl] = _dot(x, w_ref[:, sl]).astype(BF16)


def _two_group_specs(tm, n_prompt_tiles):
    last = n_prompt_tiles - 1
    return (pl.BlockSpec((tm, D_MODEL), lambda i: (jnp.minimum(i, last), 0)),
            pl.BlockSpec((tm, D_MODEL), lambda i: (jnp.maximum(i - n_prompt_tiles, 0), 0)))


def _inproj(xp, xs, w_in):
    npt = T_PROMPT // TM_IN
    spec_p, spec_s = _two_group_specs(TM_IN, npt)
    return pl.pallas_call(
        functools.partial(_inproj_kernel, n_prompt_tiles=npt),
        grid=(T_ALL // TM_IN,),
        in_specs=[spec_p, spec_s, _const_spec((D_MODEL, D_IN))],
        out_specs=pl.BlockSpec((TM_IN, D_IN), lambda i: (i, 0)),
        out_shape=jax.ShapeDtypeStruct((T_ALL, D_IN), BF16),
        compiler_params=_cparams(), name="inproj")(xp, xs, w_in)


def _seq_position(i):
    per_prompt = SEQ // TM_SEQ
    n_prompt = T_PROMPT // TM_SEQ
    is_prompt = i < n_prompt
    pos = jnp.where(is_prompt, lax.rem(i, per_prompt), i - n_prompt)
    nblk = jnp.where(is_prompt, per_prompt, DEC_SEQ // TM_SEQ)
    return pos, nblk


def _pool_kernel(u_ref, prev_ref, next_ref, a_ref, wp_ref, sc_ref, o_ref):
    pos, nblk = _seq_position(pl.program_id(0))
    cur = u_ref[...]
    zero = jnp.zeros((POOL_HALO, D_POOL), BF16)
    prev = jnp.where(pos == 0, zero, prev_ref[...])
    nxt = jnp.where(pos == nblk - 1, zero, next_ref[...])
    ext = jnp.concatenate([prev, cur, nxt], axis=0)
    t = pos * TM_SEQ + lax.broadcasted_iota(jnp.int32, (TM_SEQ, 1), 0)
    seq_len = nblk * TM_SEQ
    for g, w in enumerate(POOL_WINDOWS):
        sl = slice(g * POOL_GROUP, (g + 1) * POOL_GROUP)
        win_sum = _dot(a_ref[g], ext[:, sl])
        cnt = (jnp.minimum(t - w // 2 + w, seq_len) - jnp.maximum(t - w // 2, 0)).astype(F32)
        p = win_sum / cnt - cur[:, sl].astype(F32)
        y = _dot(p.astype(BF16), wp_ref[g]) * sc_ref[:, sl]
        o_ref[:, sl] = y.astype(BF16)


def _pool_band_matrices():
    t = np.arange(TM_SEQ)[:, None]
    c = np.arange(TM_SEQ + 2 * POOL_HALO)[None, :] - POOL_HALO
    mats = [((c - t >= -(w // 2)) & (c - t <= w // 2 - 1)) for w in POOL_WINDOWS]
    return jnp.asarray(np.stack(mats).astype(np.float32), dtype=BF16)


def _pool_mixer(u, w_pool, pool_scale):
    per_halo = TM_SEQ // POOL_HALO
    n_halo = T_ALL // POOL_HALO
    return pl.pallas_call(
        _pool_kernel,
        grid=(T_ALL // TM_SEQ,),
        in_specs=[
            pl.BlockSpec((TM_SEQ, D_POOL), lambda i: (i, 0)),
            pl.BlockSpec((POOL_HALO, D_POOL), lambda i: (jnp.maximum(i * per_halo - 1, 0), 0)),
            pl.BlockSpec((POOL_HALO, D_POOL), lambda i: (jnp.minimum((i + 1) * per_halo, n_halo - 1), 0)),
            _const_spec((len(POOL_WINDOWS), TM_SEQ, TM_SEQ + 2 * POOL_HALO)),
            _const_spec((len(POOL_WINDOWS), POOL_GROUP, POOL_GROUP)),
            _const_spec((1, D_POOL)),
        ],
        out_specs=pl.BlockSpec((TM_SEQ, D_POOL), lambda i: (i, 0)),
        out_shape=jax.ShapeDtypeStruct((T_ALL, D_POOL), BF16),
        compiler_params=_cparams(), name="pool_mixer")(
            u, u, u, _pool_band_matrices(), w_pool, pool_scale)


ROWS_PER_STEP = TM_SEQ // GRID_W
KV_WINDOW_ROWS = 2 * ROWS_PER_STEP


def _na_step_geometry(s):
    steps_per_prompt = SEQ // TM_SEQ
    n_prompt_steps = T_PROMPT // TM_SEQ
    is_prompt = s < n_prompt_steps
    jj = jnp.where(is_prompt, lax.rem(s, steps_per_prompt), s - n_prompt_steps)
    rows = jnp.where(is_prompt, SEQ // GRID_W, DEC_SEQ // GRID_W)
    seq_row0 = jnp.where(is_prompt, (s // steps_per_prompt) * (SEQ // GRID_W), T_PROMPT // GRID_W)
    return jj * ROWS_PER_STEP, rows, seq_row0


def _na_window_row(s):
    r0, rows, seq_row0 = _na_step_geometry(s)
    return jnp.clip(r0 - NA_ROWS // 2, 0, rows - KV_WINDOW_ROWS), seq_row0


def _na_kernel(q_ref, k_ref, v_ref, bias_ref, o_ref):
    s = pl.program_id(0)
    r0, rows, _ = _na_step_geometry(s)
    win0, _ = _na_window_row(s)
    gl = HEADS_PER_GROUP * NA_HEAD_DIM
    lane = lax.broadcasted_iota(jnp.int32, (GRID_W, gl), 1)
    n_keys = NA_ROWS * GRID_W

    def row_body(i, carry):
        r = r0 + i
        row_start = jnp.clip(r - NA_ROWS // 2, 0, rows - NA_ROWS)
        k0 = pl.multiple_of((row_start - win0) * GRID_W, GRID_W)
        d0 = NA_ROWS - 1 + row_start - r
        q0 = pl.multiple_of(i * GRID_W, GRID_W)
        for p in range(NA_HEADS // HEADS_PER_GROUP):
            ls = slice(p * gl, (p + 1) * gl)
            q2 = q_ref[pl.ds(q0, GRID_W), ls] * jnp.asarray(NA_HEAD_DIM ** -0.5, BF16)
            zq = jnp.zeros_like(q2)
            qs = jnp.concatenate(
                [jnp.where((lane >= h * NA_HEAD_DIM) & (lane < (h + 1) * NA_HEAD_DIM), q2, zq)
                 for h in range(HEADS_PER_GROUP)], axis=0)
            kw = k_ref[pl.ds(k0, n_keys), ls]
            vw = v_ref[pl.ds(k0, n_keys), ls]
            sc = _dot_nt(qs, kw)
            sc = jnp.concatenate(
                [sc[:, j * LANES:(j + 1) * LANES] + bias_ref[p, d0 + 2 * j] for j in range(n_keys // LANES)],
                axis=1)
            m = jnp.max(sc, axis=-1, keepdims=True)
            e = jnp.exp(sc - m)
            den = jnp.sum(e, axis=-1, keepdims=True)
            o = _dot(e.astype(BF16), vw) / den
            out = o[0:GRID_W]
            for h in range(1, HEADS_PER_GROUP):
                out = jnp.where(lane >= h * NA_HEAD_DIM, o[h * GRID_W:(h + 1) * GRID_W], out)
            o_ref[pl.ds(q0, GRID_W), ls] = out.astype(BF16)
        return carry

    lax.fori_loop(0, ROWS_PER_STEP, row_body, 0, unroll=2)


def _na_bias_table(rpb):
    qc = np.arange(GRID_W)[:, None]
    kc = np.arange(GRID_W)[None, :]
    dc = np.clip(kc - qc + NA_COLS - 1, 0, 2 * NA_COLS - 2)
    col_start = np.clip(qc - NA_COLS // 2, 0, GRID_W - NA_COLS)
    mask = (kc - col_start >= 0) & (kc - col_start < NA_COLS)
    tab = jnp.where(mask[None, None], rpb[:, :, dc].astype(F32), NEG_INF)
    n_dr = 2 * NA_ROWS - 1
    ng = NA_HEADS // HEADS_PER_GROUP
    tab = tab.reshape(ng, HEADS_PER_GROUP, n_dr, GRID_W, GRID_W).transpose(0, 2, 1, 3, 4)
    tab = tab.reshape(ng, n_dr, HEADS_PER_GROUP * GRID_W, GRID_W)
    return jnp.concatenate([tab[:, :-1], tab[:, 1:]], axis=-1)


def _neighbourhood_attention(u, rpb):
    def kv_spec(col_block):
        def index(s):
            win0, seq_row0 = _na_window_row(s)
            return ((seq_row0 + win0) * GRID_W, col_block * D_NA)
        return pl.BlockSpec((pl.Element(KV_WINDOW_ROWS * GRID_W), pl.Element(D_NA)), index)

    bias = _na_bias_table(rpb)
    return pl.pallas_call(
        _na_kernel,
        grid=(T_ALL // TM_SEQ,),
        in_specs=[pl.BlockSpec((TM_SEQ, D_NA), lambda s: (s, 1)),
                  kv_spec(2), kv_spec(3), _const_spec(bias.shape)],
        out_specs=pl.BlockSpec((TM_SEQ, D_NA), lambda s: (s, 0)),
        out_shape=jax.ShapeDtypeStruct((T_ALL, D_NA), BF16),
        compiler_params=_cparams(), name="neighbourhood_attention")(u, u, u, bias)


def _mix_ln_kernel(ya_ref, yb_ref, xp_ref, xs_ref, w_ref, g_ref, b_ref, o_ref, *, n_prompt_tiles):
    i = pl.program_id(0)
    x = jnp.where(i < n_prompt_tiles, xp_ref[...], xs_ref[...])
    mix = _dot(ya_ref[...], w_ref[:D_POOL, :]) + _dot(yb_ref[...], w_ref[D_POOL:, :])
    o_ref[...] = _layer_norm(DEEPNORM_ALPHA * x + mix, g_ref[...], b_ref[...])


def _mix_ln(ya, yb, xp, xs, w_out, g, b):
    npt = T_PROMPT // TM_ROW
    spec_p, spec_s = _two_group_specs(TM_ROW, npt)
    return pl.pallas_call(
        functools.partial(_mix_ln_kernel, n_prompt_tiles=npt),
        grid=(T_ALL // TM_ROW,),
        in_specs=[pl.BlockSpec((TM_ROW, D_POOL), lambda i: (i, 0)),
                  pl.BlockSpec((TM_ROW, D_NA), lambda i: (i, 0)),
                  spec_p, spec_s,
                  _const_spec((D_MODEL, D_MODEL)), _const_spec((1, D_MODEL)), _const_spec((1, D_MODEL))],
        out_specs=pl.BlockSpec((TM_ROW, D_MODEL), lambda i: (i, 0)),
        out_shape=jax.ShapeDtypeStruct((T_ALL, D_MODEL), F32),
        compiler_params=_cparams(), name="mix_ln1")(ya, yb, xp, xs, w_out, g, b)


def _kv_kernel(m_ref, w_ref, o_ref):
    o_ref[...] = _dot(m_ref[...], w_ref[...]).astype(BF16)


def _memory_kv(mem, w_xkv):
    n_mem_rows = mem.shape[0]
    tn = 1024
    return pl.pallas_call(
        _kv_kernel,
        grid=(2 * D_MODEL // tn,),
        in_specs=[_const_spec((n_mem_rows, D_MODEL)), pl.BlockSpec((D_MODEL, tn), lambda j: (0, j))],
        out_specs=pl.BlockSpec((n_mem_rows, tn), lambda j: (0, j)),
        out_shape=jax.ShapeDtypeStruct((n_mem_rows, 2 * D_MODEL), BF16),
        compiler_params=_cparams(), name="memory_kv")(mem, w_xkv)


def _xattn_kernel(x_ref, wq_ref, k_ref, v_ref, o_ref):
    q = _dot(x_ref[...].astype(BF16), wq_ref[...]).astype(BF16)
    for h in range(XA_HEADS):
        sl = slice(h * XA_HEAD_DIM, (h + 1) * XA_HEAD_DIM)
        sc = _dot_nt(q[:, sl], k_ref[:, sl]) * (XA_HEAD_DIM ** -0.5)
        m = jnp.max(sc, axis=-1, keepdims=True)
        e = jnp.exp(sc - m)
        den = jnp.sum(e, axis=-1, keepdims=True)
        o_ref[:, sl] = (_dot(e.astype(BF16), v_ref[:, sl]) / den).astype(BF16)


def _mem_batch(i):
    return jnp.minimum((i * TM_ROW) // SEQ, BATCH)


def _cross_attention(x1, w_xq, kv):
    return pl.pallas_call(
        _xattn_kernel,
        grid=(T_ALL // TM_ROW,),
        in_specs=[pl.BlockSpec((TM_ROW, D_MODEL), lambda i: (i, 0)),
                  _const_spec((D_MODEL, D_MODEL)),
                  pl.BlockSpec((N_MEM, D_MODEL), lambda i: (_mem_batch(i), 0)),
                  pl.BlockSpec((N_MEM, D_MODEL), lambda i: (_mem_batch(i), 1))],
        out_specs=pl.BlockSpec((TM_ROW, D_MODEL), lambda i: (i, 0)),
        out_shape=jax.ShapeDtypeStruct((T_ALL, D_MODEL), BF16),
        compiler_params=_cparams(), name="cross_attention")(x1, w_xq, kv, kv)


def _route_kernel(a_ref, x1_ref, wo_ref, g_ref, b_ref, wr_ref, br_ref,
                  x2_ref, x2p_ref, idx_ref, gate_ref, rank_ref, cnt_ref, carry_ref):
    @pl.when(pl.program_id(0) == 0)
    def _():
        carry_ref[...] = jnp.zeros_like(carry_ref)

    x2 = _layer_norm(DEEPNORM_ALPHA * x1_ref[...] + _dot(a_ref[...], wo_ref[...]), g_ref[...], b_ref[...])
    x2_ref[...] = x2
    half = D_MODEL // 2
    x2p_ref[...] = pltpu.pack_elementwise([x2[:, :half], x2[:, half:]], packed_dtype=BF16)

    lane = lax.broadcasted_iota(jnp.int32, (TM_ROW, LANES), 1)
    lanef = lane.astype(F32)
    logits = jnp.where(lane < N_EXPERTS, _dot(x2.astype(BF16), wr_ref[...]) + br_ref[...], -jnp.inf)
    top_v, top_i, hots = [], [], []
    for _ in range(TOP_K):
        m = jnp.max(logits, axis=-1, keepdims=True)
        first = jnp.min(jnp.where(logits == m, lanef, float(LANES)), axis=-1, keepdims=True)
        hot = lanef == first
        top_v.append(m)
        top_i.append(first)
        hots.append(hot)
        logits = jnp.where(hot, -jnp.inf, logits)
    ex = [jnp.exp(v - top_v[0]) for v in top_v]
    den = ex[0] + ex[1] + ex[2] + ex[3]

    chosen = jnp.zeros((TM_ROW, LANES), F32)
    for hot in hots:
        chosen = jnp.where(hot, 1.0, chosen)
    r_i = lax.broadcasted_iota(jnp.int32, (TM_ROW, TM_ROW), 0)
    c_i = lax.broadcasted_iota(jnp.int32, (TM_ROW, TM_ROW), 1)
    before = jnp.where(c_i < r_i, 1.0, 0.0).astype(BF16)
    base = carry_ref[...] + _dot(before, chosen.astype(BF16))

    idx_out = jnp.zeros((TM_ROW, LANES), jnp.int32)
    gate_out = jnp.zeros((TM_ROW, LANES), F32)
    rank_out = jnp.zeros((TM_ROW, LANES), jnp.int32)
    for k in range(TOP_K):
        rank_k = jnp.sum(jnp.where(hots[k], base, 0.0), axis=-1, keepdims=True)
        idx_out = jnp.where(lane == k, top_i[k].astype(jnp.int32), idx_out)
        gate_out = jnp.where(lane == k, ex[k] / den, gate_out)
        rank_out = jnp.where(lane == k, rank_k.astype(jnp.int32), rank_out)
    idx_ref[...] = idx_out
    gate_ref[...] = gate_out
    rank_ref[...] = rank_out
    carry_ref[...] = carry_ref[...] + jnp.sum(chosen, axis=0, keepdims=True)
    cnt_ref[...] = carry_ref[...]


def _project_norm_route(attn, x1, w_xo, g, b, w_router, b_router):
    row = lambda w: pl.BlockSpec((TM_ROW, w), lambda i: (i, 0))
    return pl.pallas_call(
        _route_kernel,
        grid=(T_ALL // TM_ROW,),
        in_specs=[row(D_MODEL), row(D_MODEL), _const_spec((D_MODEL, D_MODEL)),
                  _const_spec((1, D_MODEL)), _const_spec((1, D_MODEL)),
                  _const_spec((D_MODEL, LANES)), _const_spec((1, LANES))],
        out_specs=[row(D_MODEL), row(D_MODEL // 2), row(LANES), row(LANES), row(LANES),
                   pl.BlockSpec((1, LANES), lambda i: (0, 0))],
        out_shape=[jax.ShapeDtypeStruct((T_ALL, D_MODEL), F32),
                   jax.ShapeDtypeStruct((T_ALL, D_MODEL // 2), jnp.uint32),
                   jax.ShapeDtypeStruct((T_ALL, LANES), jnp.int32),
                   jax.ShapeDtypeStruct((T_ALL, LANES), F32),
                   jax.ShapeDtypeStruct((T_ALL, LANES), jnp.int32),
                   jax.ShapeDtypeStruct((1, LANES), F32)],
        scratch_shapes=[pltpu.VMEM((1, LANES), F32)],
        compiler_params=_cparams(), name="proj_ln2_route")(attn, x1, w_xo, g, b, w_router, b_router)


def _gather_kernel(tok_ref, x2p_ref, o_ref, sem):
    def issue(q, carry):
        for r in range(SUBLANES):
            j = q * SUBLANES + r
            pltpu.make_async_copy(x2p_ref.at[pl.ds(tok_ref[j], 1)], o_ref.at[pl.ds(j, 1)],
                                  sem).start(priority=r % 2)
        return carry

    lax.fori_loop(0, TS_GATHER // SUBLANES, issue, 0)
    pltpu.make_async_copy(x2p_ref.at[pl.ds(0, TS_GATHER)], o_ref, sem).wait()


def _gather_rows(slot_tok, x2p):
    return pl.pallas_call(
        _gather_kernel,
        grid=(N_SLOTS // TS_GATHER,),
        in_specs=[pl.BlockSpec((TS_GATHER,), lambda i: (i,), memory_space=pltpu.SMEM),
                  pl.BlockSpec(memory_space=pl.ANY)],
        out_specs=pl.BlockSpec((TS_GATHER, D_MODEL // 2), lambda i: (i, 0)),
        out_shape=jax.ShapeDtypeStruct((N_SLOTS, D_MODEL // 2), jnp.uint32),
        scratch_shapes=[pltpu.SemaphoreType.DMA],
        compiler_params=_cparams(), name="gather_rows")(slot_tok, x2p)


def _unpack_rows(xp):
    lo = pltpu.unpack_elementwise(xp, index=0, packed_dtype=BF16, unpacked_dtype=F32)
    hi = pltpu.unpack_elementwise(xp, index=1, packed_dtype=BF16, unpacked_dtype=F32)
    return lo.astype(BF16), hi.astype(BF16)


STEP_VALID = 1
STEP_NEW_TILE = 2


def _gate_up_kernel(blk_ref, exp_ref, wcol_ref, ocol_ref, flag_ref, x_ref, wg_ref, wu_ref, bg_ref, bu_ref,
                    o_ref, wg_bf_ref, wu_bf_ref):
    flags = flag_ref[pl.program_id(0)]

    @pl.when((flags & STEP_NEW_TILE) != 0)
    def _():
        wg_bf_ref[...] = wg_ref[...].astype(BF16)
        wu_bf_ref[...] = wu_ref[...].astype(BF16)

    @pl.when((flags & STEP_VALID) != 0)
    def _():
        half = D_MODEL // 2
        x_lo, x_hi = _unpack_rows(x_ref[...])
        gate = _dot(x_lo, wg_bf_ref[:half, :]) + _dot(x_hi, wg_bf_ref[half:, :]) + bg_ref[...]
        up = _dot(x_lo, wu_bf_ref[:half, :]) + _dot(x_hi, wu_bf_ref[half:, :]) + bu_ref[...]
        gate = jnp.minimum(gate, SWIGLU_LIMIT)
        up = jnp.clip(up, -SWIGLU_LIMIT, SWIGLU_LIMIT)
        o_ref[...] = (gate * jax.nn.sigmoid(SWIGLU_ALPHA * gate) * (up + 1.0)).astype(BF16)

    @pl.when((flags & STEP_VALID) == 0)
    def _():
        o_ref[...] = jnp.zeros_like(o_ref)


def _down_kernel(blk_ref, exp_ref, wcol_ref, ocol_ref, flag_ref, a_ref, w_ref, b_ref, o_ref, w_bf_ref):
    flags = flag_ref[pl.program_id(0)]

    @pl.when((flags & STEP_NEW_TILE) != 0)
    def _():
        w_bf_ref[...] = w_ref[...].astype(BF16)

    @pl.when((flags & STEP_VALID) != 0)
    def _():
        o_ref[...] = _dot(a_ref[...], w_bf_ref[...]) + b_ref[...]

    @pl.when((flags & STEP_VALID) == 0)
    def _():
        o_ref[...] = jnp.zeros_like(o_ref)


def _pick(table, index):
    hot = index[..., None] == jnp.arange(table.shape[0], dtype=index.dtype)
    return jnp.sum(jnp.where(hot, table, 0), axis=-1)


def _expert_schedule(blocks_per_expert, n_col_tiles):
    n_steps = N_BLOCKS * n_col_tiles
    blk_end = jnp.cumsum(blocks_per_expert)
    blk_start = blk_end - blocks_per_expert
    total = blk_end[-1] * n_col_tiles
    step = jnp.arange(n_steps, dtype=jnp.int32)
    valid = step < total
    s = jnp.minimum(step, total - 1)
    e = jnp.sum((s[:, None] >= (blk_end * n_col_tiles)[None, :]).astype(jnp.int32), axis=1)
    start = _pick(blk_start, e)
    nb = _pick(blocks_per_expert, e)
    local = s - start * n_col_tiles
    col = local // nb
    blk = start + local - col * nb
    tile_id = e * n_col_tiles + col
    new_tile = valid & jnp.concatenate([jnp.ones((1,), bool), tile_id[1:] != tile_id[:-1]])
    flags = valid.astype(jnp.int32) * STEP_VALID + new_tile.astype(jnp.int32) * STEP_NEW_TILE
    spare = step - total
    blk = jnp.where(valid, blk, blk_end[-1] + spare // n_col_tiles)
    out_col = jnp.where(valid, col, spare % n_col_tiles)
    i32 = lambda a: a.astype(jnp.int32)
    return i32(blk), i32(e), i32(col), i32(out_col), flags


def _experts(xs, blocks_per_expert, w_gu, b_gu, w_down, b_down):
    rows = lambda s, blk, ex, wc, oc, fl: (blk[s], 0)
    out = lambda s, blk, ex, wc, oc, fl: (blk[s], oc[s])

    def weight(col_offset):
        return lambda s, blk, ex, wc, oc, fl: (ex[s], 0, col_offset + wc[s])

    n_gu = D_EXPERT // TN_GU
    sched = _expert_schedule(blocks_per_expert, n_gu)
    act = pl.pallas_call(
        _gate_up_kernel,
        grid_spec=pltpu.PrefetchScalarGridSpec(
            num_scalar_prefetch=5, grid=(N_BLOCKS * n_gu,),
            in_specs=[
                pl.BlockSpec((TM_EXP, D_MODEL // 2), rows),
                pl.BlockSpec((None, D_MODEL, TN_GU), weight(0)),
                pl.BlockSpec((None, D_MODEL, TN_GU), weight(n_gu)),
                pl.BlockSpec((None, 1, TN_GU), weight(0)),
                pl.BlockSpec((None, 1, TN_GU), weight(n_gu)),
            ],
            out_specs=pl.BlockSpec((TM_EXP, TN_GU), out),
            scratch_shapes=[pltpu.VMEM((D_MODEL, TN_GU), BF16), pltpu.VMEM((D_MODEL, TN_GU), BF16)]),
        out_shape=jax.ShapeDtypeStruct((N_SLOTS, D_EXPERT), BF16),
        compiler_params=_cparams(), name="expert_gate_up")(*sched, xs, w_gu, w_gu, b_gu, b_gu)

    n_dn = D_MODEL // TN_DOWN
    sched = _expert_schedule(blocks_per_expert, n_dn)
    return pl.pallas_call(
        _down_kernel,
        grid_spec=pltpu.PrefetchScalarGridSpec(
            num_scalar_prefetch=5, grid=(N_BLOCKS * n_dn,),
            in_specs=[
                pl.BlockSpec((TM_EXP, D_EXPERT), rows),
                pl.BlockSpec((None, D_EXPERT, TN_DOWN), weight(0)),
                pl.BlockSpec((None, 1, TN_DOWN), weight(0)),
            ],
            out_specs=pl.BlockSpec((TM_EXP, TN_DOWN), out),
            scratch_shapes=[pltpu.VMEM((D_EXPERT, TN_DOWN), BF16)]),
        out_shape=jax.ShapeDtypeStruct((N_SLOTS, D_MODEL), F32),
        compiler_params=_cparams(), name="expert_down")(*sched, act, w_down, b_down)


def _combine_kernel(dest_ref, gate_ref, x2_ref, y_ref, g_ref, b_ref, o_ref, buf_ref, sem):
    def issue(q, carry):
        for r in range(SUBLANES):
            t = q * SUBLANES + r
            for k in range(TOP_K):
                pltpu.make_async_copy(y_ref.at[pl.ds(dest_ref[t * TOP_K + k], 1)],
                                      buf_ref.at[k, pl.ds(t, 1)], sem).start(priority=k % 2)
        return carry

    lax.fori_loop(0, TT_COMB // SUBLANES, issue, 0)
    for k in range(TOP_K):
        pltpu.make_async_copy(y_ref.at[pl.ds(0, TT_COMB)], buf_ref.at[k], sem).wait()
    gates = gate_ref[...]
    h = gates[:, 0:1] * buf_ref[0]
    for k in range(1, TOP_K):
        h = h + gates[:, k:k + 1] * buf_ref[k]
    o_ref[...] = _layer_norm(DEEPNORM_ALPHA * x2_ref[...] + h, g_ref[...], b_ref[...])


def _combine(dest_flat, gates, x2, y_slots, g, b, first_tile, n_tiles):
    n = TT_COMB * TOP_K
    return pl.pallas_call(
        _combine_kernel,
        grid=(n_tiles,),
        in_specs=[pl.BlockSpec((n,), lambda i: (i + first_tile,), memory_space=pltpu.SMEM),
                  pl.BlockSpec((TT_COMB, LANES), lambda i: (i + first_tile, 0)),
                  pl.BlockSpec((TT_COMB, D_MODEL), lambda i: (i + first_tile, 0)),
                  pl.BlockSpec(memory_space=pl.ANY),
                  _const_spec((1, D_MODEL)), _const_spec((1, D_MODEL))],
        out_specs=pl.BlockSpec((TT_COMB, D_MODEL), lambda i: (i, 0)),
        out_shape=jax.ShapeDtypeStruct((n_tiles * TT_COMB, D_MODEL), F32),
        scratch_shapes=[pltpu.VMEM((TOP_K, TT_COMB, D_MODEL), F32), pltpu.SemaphoreType.DMA],
        compiler_params=_cparams(), name="combine_ln3")(dest_flat, gates, x2, y_slots, g, b)


def kernel(x_prompt, x_sample, mem_prompt, mem_sample, w_in, w_pool, pool_scale, rpb, w_out,
           ln1_g, ln1_b, w_xq, w_xkv, w_xo, ln2_g, ln2_b,
           w_router, b_router, w_gu, b_gu, w_down, b_down, ln3_g, ln3_b):
    assert w_in.shape[0] == 1, "single-layer problem"
    xp = x_prompt.reshape(T_PROMPT, D_MODEL)
    xs = x_sample.reshape(T_SAMPLE, D_MODEL)
    mem = jnp.concatenate([mem_prompt.reshape(BATCH * N_MEM, D_MODEL),
                           mem_sample.reshape(N_MEM, D_MODEL)], axis=0).astype(BF16)
    row = lambda v: v.reshape(1, -1).astype(F32)

    u = _inproj(xp, xs, w_in[0].astype(BF16))
    ya = _pool_mixer(u, w_pool[0].astype(BF16), row(pool_scale[0]))
    yb = _neighbourhood_attention(u, rpb[0])
    x1 = _mix_ln(ya, yb, xp, xs, w_out[0].astype(BF16), row(ln1_g[0]), row(ln1_b[0]))

    kv = _memory_kv(mem, w_xkv[0].astype(BF16))
    attn = _cross_attention(x1, w_xq[0].astype(BF16), kv)
    w_r = jnp.pad(w_router[0], ((0, 0), (0, LANES - N_EXPERTS))).astype(BF16)
    b_r = jnp.pad(b_router[0].astype(F32), (0, LANES - N_EXPERTS)).reshape(1, LANES)
    x2, x2p, top_i, gates, rank, counts = _project_norm_route(
        attn, x1, w_xo[0].astype(BF16), row(ln2_g[0]), row(ln2_b[0]), w_r, b_r)

    counts = counts[0, :N_EXPERTS].astype(jnp.int32)
    blocks_per_expert = (counts + TM_EXP - 1) // TM_EXP
    blk_end = jnp.cumsum(blocks_per_expert)
    slot_start = (blk_end - blocks_per_expert) * TM_EXP
    dest = (_pick(slot_start, top_i[:, :TOP_K]) + rank[:, :TOP_K]).reshape(T_ALL * TOP_K)
    slot_tok = jnp.zeros((N_SLOTS,), jnp.int32).at[dest].set(
        jnp.arange(T_ALL * TOP_K, dtype=jnp.int32) // TOP_K, unique_indices=True)

    xs_slots = _gather_rows(slot_tok, x2p)
    y_slots = _experts(xs_slots, blocks_per_expert, w_gu[0], b_gu[0].reshape(N_EXPERTS, 1, -1),
                       w_down[0], b_down[0].reshape(N_EXPERTS, 1, -1))

    g3, b3 = row(ln3_g[0]), row(ln3_b[0])
    n_p = T_PROMPT // TT_COMB
    y_prompt = _combine(dest, gates, x2, y_slots, g3, b3, 0, n_p)
    y_sample = _combine(dest, gates, x2, y_slots, g3, b3, n_p, T_SAMPLE // TT_COMB)
    return (y_prompt.reshape(BATCH, SEQ, D_MODEL), y_sample.reshape(1, DEC_SEQ, D_MODEL))
```

```python
import functools

import numpy as np
import jax
import jax.numpy as jnp
from jax import lax
from jax.experimental import pallas as pl
from jax.experimental.pallas import tpu as pltpu

F32 = jnp.float32
BF16 = jnp.bfloat16

D_MODEL = 2048
BATCH, SEQ = 4, 4096
DEC_SEQ = 8192
T_PROMPT = BATCH * SEQ
T_SAMPLE = DEC_SEQ
T_ALL = T_PROMPT + T_SAMPLE
GRID_W = 64
D_POOL = 1024
POOL_WINDOWS = (2, 4, 8, 16)
POOL_GROUP = 256
D_NA = 1024
NA_HEADS = 16
NA_HEAD_DIM = 64
NA_ROWS = 8
NA_COLS = 16
D_IN = D_POOL + 3 * D_NA
N_MEM = 256
XA_HEADS = 4
XA_HEAD_DIM = 512
N_EXPERTS = 32
TOP_K = 4
D_EXPERT = 2048
SWIGLU_LIMIT = 7.0
SWIGLU_ALPHA = 1.702
LN_EPS = 1e-5
DEEPNORM_ALPHA = 2.0 ** 0.25
NEG_INF = -1e30

LANES = 128
SUBLANES = 8
ROW_TILE = D_MODEL // 2 // LANES
DMA_GROUP = 8
VMEM_LIMIT = 56 * 1024 * 1024
TM_IN = 512
TM_SEQ = 512
POOL_HALO = 16
TM_ROW = 256
TM_EXP = 512
TN_GU = 512
TN_DOWN = 1024
TS_GATHER = 1024
TT_COMB = 256
HEADS_PER_GROUP = 4
N_SLOTS = T_ALL * TOP_K + N_EXPERTS * TM_EXP
N_BLOCKS = N_SLOTS // TM_EXP


def _cparams(n_axes=1):
    return pltpu.CompilerParams(dimension_semantics=("arbitrary",) * n_axes,
                                vmem_limit_bytes=VMEM_LIMIT)


def _dot(a, b):
    return jnp.dot(a, b, preferred_element_type=F32)


def _dot_nt(a, b):
    return lax.dot_general(a, b, (((1,), (1,)), ((), ())), preferred_element_type=F32)


def _layer_norm(x, g, b):
    mu = jnp.mean(x, axis=-1, keepdims=True)
    xc = x - mu
    var = jnp.mean(xc * xc, axis=-1, keepdims=True)
    return xc * lax.rsqrt(var + LN_EPS) * g + b


def _const_spec(shape):
    nd = len(shape)
    return pl.BlockSpec(shape, lambda *_: (0,) * nd, pipeline_mode=pl.Buffered(1))


def _inproj_kernel(xp_ref, xs_ref, w_ref, o_ref, *, n_prompt_tiles):
    i = pl.program_id(0)
    x = jnp.where(i < n_prompt_tiles, xp_ref[...], xs_ref[...]).astype(BF16)
    for c in range(D_IN // 1024):
        sl = slice(c * 1024, (c + 1) * 1024)
        o_ref[:, sl] = _dot(x, w_ref[:, sl]).astype(BF16)


def _two_group_specs(tm, n_prompt_tiles):
    last = n_prompt_tiles - 1
    return (pl.BlockSpec((tm, D_MODEL), lambda i: (jnp.minimum(i, last), 0)),
            pl.BlockSpec((tm, D_MODEL), lambda i: (jnp.maximum(i - n_prompt_tiles, 0), 0)))


def _inproj(xp, xs, w_in):
    npt = T_PROMPT // TM_IN
    spec_p, spec_s = _two_group_specs(TM_IN, npt)
    return pl.pallas_call(
        functools.partial(_inproj_kernel, n_prompt_tiles=npt),
        grid=(T_ALL // TM_IN,),
        in_specs=[spec_p, spec_s, _const_spec((D_MODEL, D_IN))],
        out_specs=pl.BlockSpec((TM_IN, D_IN), lambda i: (i, 0)),
        out_shape=jax.ShapeDtypeStruct((T_ALL, D_IN), BF16),
        compiler_params=_cparams(), name="inproj")(xp, xs, w_in)


def _seq_position(i):
    per_prompt = SEQ // TM_SEQ
    n_prompt = T_PROMPT // TM_SEQ
    is_prompt = i < n_prompt
    pos = jnp.where(is_prompt, lax.rem(i, per_prompt), i - n_prompt)
    nblk = jnp.where(is_prompt, per_prompt, DEC_SEQ // TM_SEQ)
    return pos, nblk


def _pool_kernel(u_ref, prev_ref, next_ref, a_ref, wp_ref, sc_ref, o_ref):
    pos, nblk = _seq_position(pl.program_id(0))
    cur = u_ref[...]
    zero = jnp.zeros((POOL_HALO, D_POOL), BF16)
    prev = jnp.where(pos == 0, zero, prev_ref[...])
    nxt = jnp.where(pos == nblk - 1, zero, next_ref[...])
    ext = jnp.concatenate([prev, cur, nxt], axis=0)
    t = pos * TM_SEQ + lax.broadcasted_iota(jnp.int32, (TM_SEQ, 1), 0)
    seq_len = nblk * TM_SEQ
    for g, w in enumerate(POOL_WINDOWS):
        sl = slice(g * POOL_GROUP, (g + 1) * POOL_GROUP)
        win_sum = _dot(a_ref[g], ext[:, sl])
        cnt = (jnp.minimum(t - w // 2 + w, seq_len) - jnp.maximum(t - w // 2, 0)).astype(F32)
        p = win_sum / cnt - cur[:, sl].astype(F32)
        y = _dot(p.astype(BF16), wp_ref[g]) * sc_ref[:, sl]
        o_ref[:, sl] = y.astype(BF16)


def _pool_band_matrices():
    t = np.arange(TM_SEQ)[:, None]
    c = np.arange(TM_SEQ + 2 * POOL_HALO)[None, :] - POOL_HALO
    mats = [((c - t >= -(w // 2)) & (c - t <= w // 2 - 1)) for w in POOL_WINDOWS]
    return jnp.asarray(np.stack(mats).astype(np.float32), dtype=BF16)


def _pool_mixer(u, w_pool, pool_scale):
    per_halo = TM_SEQ // POOL_HALO
    n_halo = T_ALL // POOL_HALO
    return pl.pallas_call(
        _pool_kernel,
        grid=(T_ALL // TM_SEQ,),
        in_specs=[
            pl.BlockSpec((TM_SEQ, D_POOL), lambda i: (i, 0)),
            pl.BlockSpec((POOL_HALO, D_POOL), lambda i: (jnp.maximum(i * per_halo - 1, 0), 0)),
            pl.BlockSpec((POOL_HALO, D_POOL), lambda i: (jnp.minimum((i + 1) * per_halo, n_halo - 1), 0)),
            _const_spec((len(POOL_WINDOWS), TM_SEQ, TM_SEQ + 2 * POOL_HALO)),
            _const_spec((len(POOL_WINDOWS), POOL_GROUP, POOL_GROUP)),
            _const_spec((1, D_POOL)),
        ],
        out_specs=pl.BlockSpec((TM_SEQ, D_POOL), lambda i: (i, 0)),
        out_shape=jax.ShapeDtypeStruct((T_ALL, D_POOL), BF16),
        compiler_params=_cparams(), name="pool_mixer")(
            u, u, u, _pool_band_matrices(), w_pool, pool_scale)


ROWS_PER_STEP = TM_SEQ // GRID_W
KV_WINDOW_ROWS = 2 * ROWS_PER_STEP


def _na_step_geometry(s):
    steps_per_prompt = SEQ // TM_SEQ
    n_prompt_steps = T_PROMPT // TM_SEQ
    is_prompt = s < n_prompt_steps
    jj = jnp.where(is_prompt, lax.rem(s, steps_per_prompt), s - n_prompt_steps)
    rows = jnp.where(is_prompt, SEQ // GRID_W, DEC_SEQ // GRID_W)
    seq_row0 = jnp.where(is_prompt, (s // steps_per_prompt) * (SEQ // GRID_W), T_PROMPT // GRID_W)
    return jj * ROWS_PER_STEP, rows, seq_row0


def _na_window_row(s):
    r0, rows, seq_row0 = _na_step_geometry(s)
    return jnp.clip(r0 - NA_ROWS // 2, 0, rows - KV_WINDOW_ROWS), seq_row0


def _na_kernel(q_ref, k_ref, v_ref, bias_ref, o_ref):
    s = pl.program_id(0)
    r0, rows, _ = _na_step_geometry(s)
    win0, _ = _na_window_row(s)
    gl = HEADS_PER_GROUP * NA_HEAD_DIM
    lane = lax.broadcasted_iota(jnp.int32, (GRID_W, gl), 1)
    n_keys = NA_ROWS * GRID_W

    def row_body(i, carry):
        r = r0 + i
        row_start = jnp.clip(r - NA_ROWS // 2, 0, rows - NA_ROWS)
        k0 = pl.multiple_of((row_start - win0) * GRID_W, GRID_W)
        d0 = NA_ROWS - 1 + row_start - r
        q0 = pl.multiple_of(i * GRID_W, GRID_W)
        for p in range(NA_HEADS // HEADS_PER_GROUP):
            ls = slice(p * gl, (p + 1) * gl)
            q2 = q_ref[pl.ds(q0, GRID_W), ls] * jnp.asarray(NA_HEAD_DIM ** -0.5, BF16)
            zq = jnp.zeros_like(q2)
            qs = jnp.concatenate(
                [jnp.where((lane >= h * NA_HEAD_DIM) & (lane < (h + 1) * NA_HEAD_DIM), q2, zq)
                 for h in range(HEADS_PER_GROUP)], axis=0)
            kw = k_ref[pl.ds(k0, n_keys), ls]
            vw = v_ref[pl.ds(k0, n_keys), ls]
            sc = _dot_nt(qs, kw)
            sc = jnp.concatenate(
                [sc[:, j * LANES:(j + 1) * LANES] + bias_ref[p, d0 + 2 * j] for j in range(n_keys // LANES)],
                axis=1)
            m = jnp.max(sc, axis=-1, keepdims=True)
            e = jnp.exp(sc - m)
            den = jnp.sum(e, axis=-1, keepdims=True)
            o = _dot(e.astype(BF16), vw) / den
            out = o[0:GRID_W]
            for h in range(1, HEADS_PER_GROUP):
                out = jnp.where(lane >= h * NA_HEAD_DIM, o[h * GRID_W:(h + 1) * GRID_W], out)
            o_ref[pl.ds(q0, GRID_W), ls] = out.astype(BF16)
        return carry

    lax.fori_loop(0, ROWS_PER_STEP, row_body, 0, unroll=4)


def _na_bias_table(rpb):
    qc = np.arange(GRID_W)[:, None]
    kc = np.arange(GRID_W)[None, :]
    dc = np.clip(kc - qc + NA_COLS - 1, 0, 2 * NA_COLS - 2)
    col_start = np.clip(qc - NA_COLS // 2, 0, GRID_W - NA_COLS)
    mask = (kc - col_start >= 0) & (kc - col_start < NA_COLS)
    tab = jnp.where(mask[None, None], rpb[:, :, dc].astype(F32), NEG_INF)
    n_dr = 2 * NA_ROWS - 1
    ng = NA_HEADS // HEADS_PER_GROUP
    tab = tab.reshape(ng, HEADS_PER_GROUP, n_dr, GRID_W, GRID_W).transpose(0, 2, 1, 3, 4)
    tab = tab.reshape(ng, n_dr, HEADS_PER_GROUP * GRID_W, GRID_W)
    return jnp.concatenate([tab[:, :-1], tab[:, 1:]], axis=-1)


def _neighbourhood_attention(u, rpb):
    def kv_spec(col_block):
        def index(s):
            win0, seq_row0 = _na_window_row(s)
            return ((seq_row0 + win0) * GRID_W, col_block * D_NA)
        return pl.BlockSpec((pl.Element(KV_WINDOW_ROWS * GRID_W), pl.Element(D_NA)), index)

    bias = _na_bias_table(rpb)
    return pl.pallas_call(
        _na_kernel,
        grid=(T_ALL // TM_SEQ,),
        in_specs=[pl.BlockSpec((TM_SEQ, D_NA), lambda s: (s, 1)),
                  kv_spec(2), kv_spec(3), _const_spec(bias.shape)],
        out_specs=pl.BlockSpec((TM_SEQ, D_NA), lambda s: (s, 0)),
        out_shape=jax.ShapeDtypeStruct((T_ALL, D_NA), BF16),
        compiler_params=_cparams(), name="neighbourhood_attention")(u, u, u, bias)


def _mix_ln_kernel(ya_ref, yb_ref, xp_ref, xs_ref, w_ref, g_ref, b_ref, o_ref, *, n_prompt_tiles):
    i = pl.program_id(0)
    x = jnp.where(i < n_prompt_tiles, xp_ref[...], xs_ref[...])
    mix = _dot(ya_ref[...], w_ref[:D_POOL, :]) + _dot(yb_ref[...], w_ref[D_POOL:, :])
    o_ref[...] = _layer_norm(DEEPNORM_ALPHA * x + mix, g_ref[...], b_ref[...])


def _mix_ln(ya, yb, xp, xs, w_out, g, b):
    npt = T_PROMPT // TM_ROW
    spec_p, spec_s = _two_group_specs(TM_ROW, npt)
    return pl.pallas_call(
        functools.partial(_mix_ln_kernel, n_prompt_tiles=npt),
        grid=(T_ALL // TM_ROW,),
        in_specs=[pl.BlockSpec((TM_ROW, D_POOL), lambda i: (i, 0)),
                  pl.BlockSpec((TM_ROW, D_NA), lambda i: (i, 0)),
                  spec_p, spec_s,
                  _const_spec((D_MODEL, D_MODEL)), _const_spec((1, D_MODEL)), _const_spec((1, D_MODEL))],
        out_specs=pl.BlockSpec((TM_ROW, D_MODEL), lambda i: (i, 0)),
        out_shape=jax.ShapeDtypeStruct((T_ALL, D_MODEL), F32),
        compiler_params=_cparams(), name="mix_ln1")(ya, yb, xp, xs, w_out, g, b)


def _kv_kernel(m_ref, w_ref, o_ref):
    o_ref[...] = _dot(m_ref[...], w_ref[...]).astype(BF16)


def _memory_kv(mem, w_xkv):
    n_mem_rows = mem.shape[0]
    tn = 1024
    return pl.pallas_call(
        _kv_kernel,
        grid=(2 * D_MODEL // tn,),
        in_specs=[_const_spec((n_mem_rows, D_MODEL)), pl.BlockSpec((D_MODEL, tn), lambda j: (0, j))],
        out_specs=pl.BlockSpec((n_mem_rows, tn), lambda j: (0, j)),
        out_shape=jax.ShapeDtypeStruct((n_mem_rows, 2 * D_MODEL), BF16),
        compiler_params=_cparams(), name="memory_kv")(mem, w_xkv)


def _xattn_kernel(x_ref, wq_ref, k_ref, v_ref, o_ref):
    q = _dot(x_ref[...].astype(BF16), wq_ref[...]).astype(BF16)
    for h in range(XA_HEADS):
        sl = slice(h * XA_HEAD_DIM, (h + 1) * XA_HEAD_DIM)
        sc = _dot_nt(q[:, sl], k_ref[:, sl]) * (XA_HEAD_DIM ** -0.5)
        m = jnp.max(sc, axis=-1, keepdims=True)
        e = jnp.exp(sc - m)
        den = jnp.sum(e, axis=-1, keepdims=True)
        o_ref[:, sl] = (_dot(e.astype(BF16), v_ref[:, sl]) / den).astype(BF16)


def _mem_batch(i):
    return jnp.minimum((i * TM_ROW) // SEQ, BATCH)


def _cross_attention(x1, w_xq, kv):
    return pl.pallas_call(
        _xattn_kernel,
        grid=(T_ALL // TM_ROW,),
        in_specs=[pl.BlockSpec((TM_ROW, D_MODEL), lambda i: (i, 0)),
                  _const_spec((D_MODEL, D_MODEL)),
                  pl.BlockSpec((N_MEM, D_MODEL), lambda i: (_mem_batch(i), 0)),
                  pl.BlockSpec((N_MEM, D_MODEL), lambda i: (_mem_batch(i), 1))],
        out_specs=pl.BlockSpec((TM_ROW, D_MODEL), lambda i: (i, 0)),
        out_shape=jax.ShapeDtypeStruct((T_ALL, D_MODEL), BF16),
        compiler_params=_cparams(), name="cross_attention")(x1, w_xq, kv, kv)


def _route_kernel(a_ref, x1_ref, wo_ref, g_ref, b_ref, wr_ref, br_ref,
                  x2_ref, x2p_ref, idx_ref, gate_ref, rank_ref, cnt_ref, carry_ref):
    @pl.when(pl.program_id(0) == 0)
    def _():
        carry_ref[...] = jnp.zeros_like(carry_ref)

    x2 = _layer_norm(DEEPNORM_ALPHA * x1_ref[...] + _dot(a_ref[...], wo_ref[...]), g_ref[...], b_ref[...])
    x2_ref[...] = x2
    half = D_MODEL // 2
    packed = pltpu.pack_elementwise([x2[:, :half], x2[:, half:]], packed_dtype=BF16)
    for s in range(ROW_TILE):
        x2p_ref[pl.ds(s, TM_ROW, stride=ROW_TILE), :] = packed[:, s * LANES:(s + 1) * LANES]

    lane = lax.broadcasted_iota(jnp.int32, (TM_ROW, LANES), 1)
    lanef = lane.astype(F32)
    logits = jnp.where(lane < N_EXPERTS, _dot(x2.astype(BF16), wr_ref[...]) + br_ref[...], -jnp.inf)
    top_v, top_i, hots = [], [], []
    for _ in range(TOP_K):
        m = jnp.max(logits, axis=-1, keepdims=True)
        first = jnp.min(jnp.where(logits == m, lanef, float(LANES)), axis=-1, keepdims=True)
        hot = lanef == first
        top_v.append(m)
        top_i.append(first)
        hots.append(hot)
        logits = jnp.where(hot, -jnp.inf, logits)
    ex = [jnp.exp(v - top_v[0]) for v in top_v]
    den = ex[0] + ex[1] + ex[2] + ex[3]

    chosen = jnp.zeros((TM_ROW, LANES), F32)
    for hot in hots:
        chosen = jnp.where(hot, 1.0, chosen)
    r_i = lax.broadcasted_iota(jnp.int32, (TM_ROW, TM_ROW), 0)
    c_i = lax.broadcasted_iota(jnp.int32, (TM_ROW, TM_ROW), 1)
    before = jnp.where(c_i < r_i, 1.0, 0.0).astype(BF16)
    base = carry_ref[...] + _dot(before, chosen.astype(BF16))

    idx_out = jnp.zeros((TM_ROW, LANES), jnp.int32)
    gate_out = jnp.zeros((TM_ROW, LANES), F32)
    rank_out = jnp.zeros((TM_ROW, LANES), jnp.int32)
    for k in range(TOP_K):
        rank_k = jnp.sum(jnp.where(hots[k], base, 0.0), axis=-1, keepdims=True)
        idx_out = jnp.where(lane == k, top_i[k].astype(jnp.int32), idx_out)
        gate_out = jnp.where(lane == k, ex[k] / den, gate_out)
        rank_out = jnp.where(lane == k, rank_k.astype(jnp.int32), rank_out)
    idx_ref[...] = idx_out
    gate_ref[...] = gate_out
    rank_ref[...] = rank_out
    carry_ref[...] = carry_ref[...] + jnp.sum(chosen, axis=0, keepdims=True)
    cnt_ref[...] = carry_ref[...]


def _project_norm_route(attn, x1, w_xo, g, b, w_router, b_router):
    row = lambda w: pl.BlockSpec((TM_ROW, w), lambda i: (i, 0))
    return pl.pallas_call(
        _route_kernel,
        grid=(T_ALL // TM_ROW,),
        in_specs=[row(D_MODEL), row(D_MODEL), _const_spec((D_MODEL, D_MODEL)),
                  _const_spec((1, D_MODEL)), _const_spec((1, D_MODEL)),
                  _const_spec((D_MODEL, LANES)), _const_spec((1, LANES))],
        out_specs=[row(D_MODEL), pl.BlockSpec((TM_ROW * ROW_TILE, LANES), lambda i: (i, 0)),
                   row(LANES), row(LANES), row(LANES),
                   pl.BlockSpec((1, LANES), lambda i: (0, 0))],
        out_shape=[jax.ShapeDtypeStruct((T_ALL, D_MODEL), F32),
                   jax.ShapeDtypeStruct((T_ALL * ROW_TILE, LANES), jnp.uint32),
                   jax.ShapeDtypeStruct((T_ALL, LANES), jnp.int32),
                   jax.ShapeDtypeStruct((T_ALL, LANES), F32),
                   jax.ShapeDtypeStruct((T_ALL, LANES), jnp.int32),
                   jax.ShapeDtypeStruct((1, LANES), F32)],
        scratch_shapes=[pltpu.VMEM((1, LANES), F32)],
        compiler_params=_cparams(), name="proj_ln2_route")(attn, x1, w_xo, g, b, w_router, b_router)


def _gather_kernel(tok_ref, x2p_ref, o_ref, sem):
    def issue(q, carry):
        for r in range(DMA_GROUP):
            j = q * DMA_GROUP + r
            src = pl.multiple_of(tok_ref[j] * ROW_TILE, ROW_TILE)
            dst = pl.multiple_of(j * ROW_TILE, ROW_TILE)
            pltpu.make_async_copy(x2p_ref.at[pl.ds(src, ROW_TILE)], o_ref.at[pl.ds(dst, ROW_TILE)],
                                  sem).start(priority=r % 2)
        return carry

    lax.fori_loop(0, TS_GATHER // DMA_GROUP, issue, 0)
    pltpu.make_async_copy(x2p_ref.at[pl.ds(0, TS_GATHER * ROW_TILE)], o_ref, sem).wait()


def _gather_rows(slot_tok, x2p):
    return pl.pallas_call(
        _gather_kernel,
        grid=(N_SLOTS // TS_GATHER,),
        in_specs=[pl.BlockSpec((TS_GATHER,), lambda i: (i,), memory_space=pltpu.SMEM),
                  pl.BlockSpec(memory_space=pl.ANY)],
        out_specs=pl.BlockSpec((TS_GATHER * ROW_TILE, LANES), lambda i: (i, 0)),
        out_shape=jax.ShapeDtypeStruct((N_SLOTS * ROW_TILE, LANES), jnp.uint32),
        scratch_shapes=[pltpu.SemaphoreType.DMA],
        compiler_params=_cparams(), name="gather_rows")(slot_tok, x2p)


def _unpack_rows(x_ref, n_rows):
    lo, hi = [], []
    for s in range(ROW_TILE):
        slab = x_ref[pl.ds(s, n_rows, stride=ROW_TILE), :]
        lo.append(pltpu.unpack_elementwise(slab, index=0, packed_dtype=BF16, unpacked_dtype=F32).astype(BF16))
        hi.append(pltpu.unpack_elementwise(slab, index=1, packed_dtype=BF16, unpacked_dtype=F32).astype(BF16))
    return jnp.concatenate(lo, axis=1), jnp.concatenate(hi, axis=1)


STEP_VALID = 1
STEP_NEW_TILE = 2


def _gate_up_kernel(blk_ref, exp_ref, wcol_ref, ocol_ref, flag_ref, x_ref, wg_ref, wu_ref, bg_ref, bu_ref,
                    o_ref, wg_bf_ref, wu_bf_ref):
    flags = flag_ref[pl.program_id(0)]

    @pl.when((flags & STEP_NEW_TILE) != 0)
    def _():
        wg_bf_ref[...] = wg_ref[...].astype(BF16)
        wu_bf_ref[...] = wu_ref[...].astype(BF16)

    @pl.when((flags & STEP_VALID) != 0)
    def _():
        half = D_MODEL // 2
        x_lo, x_hi = _unpack_rows(x_ref, TM_EXP)
        gate = _dot(x_lo, wg_bf_ref[:half, :]) + _dot(x_hi, wg_bf_ref[half:, :]) + bg_ref[...]
        up = _dot(x_lo, wu_bf_ref[:half, :]) + _dot(x_hi, wu_bf_ref[half:, :]) + bu_ref[...]
        gate = jnp.minimum(gate, SWIGLU_LIMIT)
        up = jnp.clip(up, -SWIGLU_LIMIT, SWIGLU_LIMIT)
        o_ref[...] = (gate * jax.nn.sigmoid(SWIGLU_ALPHA * gate) * (up + 1.0)).astype(BF16)

    @pl.when((flags & STEP_VALID) == 0)
    def _():
        o_ref[...] = jnp.zeros_like(o_ref)


def _down_kernel(blk_ref, exp_ref, wcol_ref, ocol_ref, flag_ref, a_ref, w_ref, b_ref, o_ref, w_bf_ref):
    flags = flag_ref[pl.program_id(0)]

    @pl.when((flags & STEP_NEW_TILE) != 0)
    def _():
        w_bf_ref[...] = w_ref[...].astype(BF16)

    @pl.when((flags & STEP_VALID) != 0)
    def _():
        o_ref[...] = _dot(a_ref[...], w_bf_ref[...]) + b_ref[...]

    @pl.when((flags & STEP_VALID) == 0)
    def _():
        o_ref[...] = jnp.zeros_like(o_ref)


def _pick(table, index):
    hot = index[..., None] == jnp.arange(table.shape[0], dtype=index.dtype)
    return jnp.sum(jnp.where(hot, table, 0), axis=-1)


def _expert_schedule(blocks_per_expert, n_col_tiles):
    n_steps = N_BLOCKS * n_col_tiles
    blk_end = jnp.cumsum(blocks_per_expert)
    blk_start = blk_end - blocks_per_expert
    total = blk_end[-1] * n_col_tiles
    step = jnp.arange(n_steps, dtype=jnp.int32)
    valid = step < total
    s = jnp.minimum(step, total - 1)
    e = jnp.sum((s[:, None] >= (blk_end * n_col_tiles)[None, :]).astype(jnp.int32), axis=1)
    start = _pick(blk_start, e)
    nb = _pick(blocks_per_expert, e)
    local = s - start * n_col_tiles
    col = local // nb
    blk = start + local - col * nb
    tile_id = e * n_col_tiles + col
    new_tile = valid & jnp.concatenate([jnp.ones((1,), bool), tile_id[1:] != tile_id[:-1]])
    flags = valid.astype(jnp.int32) * STEP_VALID + new_tile.astype(jnp.int32) * STEP_NEW_TILE
    spare = step - total
    blk = jnp.where(valid, blk, blk_end[-1] + spare // n_col_tiles)
    out_col = jnp.where(valid, col, spare % n_col_tiles)
    i32 = lambda a: a.astype(jnp.int32)
    return i32(blk), i32(e), i32(col), i32(out_col), flags


def _experts(xs, blocks_per_expert, w_gu, b_gu, w_down, b_down):
    rows = lambda s, blk, ex, wc, oc, fl: (blk[s], 0)
    out = lambda s, blk, ex, wc, oc, fl: (blk[s], oc[s])

    def weight(col_offset):
        return lambda s, blk, ex, wc, oc, fl: (ex[s], 0, col_offset + wc[s])

    n_gu = D_EXPERT // TN_GU
    sched = _expert_schedule(blocks_per_expert, n_gu)
    act = pl.pallas_call(
        _gate_up_kernel,
        grid_spec=pltpu.PrefetchScalarGridSpec(
            num_scalar_prefetch=5, grid=(N_BLOCKS * n_gu,),
            in_specs=[
                pl.BlockSpec((TM_EXP * ROW_TILE, LANES), rows),
                pl.BlockSpec((None, D_MODEL, TN_GU), weight(0)),
                pl.BlockSpec((None, D_MODEL, TN_GU), weight(n_gu)),
                pl.BlockSpec((None, 1, TN_GU), weight(0)),
                pl.BlockSpec((None, 1, TN_GU), weight(n_gu)),
            ],
            out_specs=pl.BlockSpec((TM_EXP, TN_GU), out),
            scratch_shapes=[pltpu.VMEM((D_MODEL, TN_GU), BF16), pltpu.VMEM((D_MODEL, TN_GU), BF16)]),
        out_shape=jax.ShapeDtypeStruct((N_SLOTS, D_EXPERT), BF16),
        compiler_params=_cparams(), name="expert_gate_up")(*sched, xs, w_gu, w_gu, b_gu, b_gu)

    n_dn = D_MODEL // TN_DOWN
    sched = _expert_schedule(blocks_per_expert, n_dn)
    return pl.pallas_call(
        _down_kernel,
        grid_spec=pltpu.PrefetchScalarGridSpec(
            num_scalar_prefetch=5, grid=(N_BLOCKS * n_dn,),
            in_specs=[
                pl.BlockSpec((TM_EXP, D_EXPERT), rows),
                pl.BlockSpec((None, D_EXPERT, TN_DOWN), weight(0)),
                pl.BlockSpec((None, 1, TN_DOWN), weight(0)),
            ],
            out_specs=pl.BlockSpec((TM_EXP, TN_DOWN), out),
            scratch_shapes=[pltpu.VMEM((D_EXPERT, TN_DOWN), BF16)]),
        out_shape=jax.ShapeDtypeStruct((N_SLOTS, D_MODEL), F32),
        compiler_params=_cparams(), name="expert_down")(*sched, act, w_down, b_down)


def _combine_kernel(dest_ref, gate_ref, x2_ref, y_ref, g_ref, b_ref, o_ref, buf_ref, sem):
    def issue(q, carry):
        for r in range(SUBLANES):
            t = q * SUBLANES + r
            for k in range(TOP_K):
                pltpu.make_async_copy(y_ref.at[pl.ds(dest_ref[t * TOP_K + k], 1)],
                                      buf_ref.at[k, q, pl.ds(r, 1)], sem).start(priority=k % 2)
        return carry

    n_tiles = TT_COMB // SUBLANES
    lax.fori_loop(0, n_tiles, issue, 0)
    for k in range(TOP_K):
        for q in range(n_tiles):
            pltpu.make_async_copy(y_ref.at[pl.ds(0, SUBLANES)], buf_ref.at[k, q], sem).wait()
    gates = gate_ref[...]
    plane = lambda k: buf_ref[k].reshape(TT_COMB, D_MODEL)
    h = gates[:, 0:1] * plane(0)
    for k in range(1, TOP_K):
        h = h + gates[:, k:k + 1] * plane(k)
    o_ref[...] = _layer_norm(DEEPNORM_ALPHA * x2_ref[...] + h, g_ref[...], b_ref[...])


def _combine(dest_flat, gates, x2, y_slots, g, b, first_tile, n_tiles):
    n = TT_COMB * TOP_K
    return pl.pallas_call(
        _combine_kernel,
        grid=(n_tiles,),
        in_specs=[pl.BlockSpec((n,), lambda i: (i + first_tile,), memory_space=pltpu.SMEM),
                  pl.BlockSpec((TT_COMB, LANES), lambda i: (i + first_tile, 0)),
                  pl.BlockSpec((TT_COMB, D_MODEL), lambda i: (i + first_tile, 0)),
                  pl.BlockSpec(memory_space=pl.ANY),
                  _const_spec((1, D_MODEL)), _const_spec((1, D_MODEL))],
        out_specs=pl.BlockSpec((TT_COMB, D_MODEL), lambda i: (i, 0)),
        out_shape=jax.ShapeDtypeStruct((n_tiles * TT_COMB, D_MODEL), F32),
        scratch_shapes=[pltpu.VMEM((TOP_K, TT_COMB // SUBLANES, SUBLANES, D_MODEL), F32),
                        pltpu.SemaphoreType.DMA],
        compiler_params=_cparams(), name="combine_ln3")(dest_flat, gates, x2, y_slots, g, b)


def kernel(x_prompt, x_sample, mem_prompt, mem_sample, w_in, w_pool, pool_scale, rpb, w_out,
           ln1_g, ln1_b, w_xq, w_xkv, w_xo, ln2_g, ln2_b,
           w_router, b_router, w_gu, b_gu, w_down, b_down, ln3_g, ln3_b):
    assert w_in.shape[0] == 1, "single-layer problem"
    xp = x_prompt.reshape(T_PROMPT, D_MODEL)
    xs = x_sample.reshape(T_SAMPLE, D_MODEL)
    mem = jnp.concatenate([mem_prompt.reshape(BATCH * N_MEM, D_MODEL),
                           mem_sample.reshape(N_MEM, D_MODEL)], axis=0).astype(BF16)
    row = lambda v: v.reshape(1, -1).astype(F32)

    u = _inproj(xp, xs, w_in[0].astype(BF16))
    ya = _pool_mixer(u, w_pool[0].astype(BF16), row(pool_scale[0]))
    yb = _neighbourhood_attention(u, rpb[0])
    x1 = _mix_ln(ya, yb, xp, xs, w_out[0].astype(BF16), row(ln1_g[0]), row(ln1_b[0]))

    kv = _memory_kv(mem, w_xkv[0].astype(BF16))
    attn = _cross_attention(x1, w_xq[0].astype(BF16), kv)
    w_r = jnp.pad(w_router[0], ((0, 0), (0, LANES - N_EXPERTS))).astype(BF16)
    b_r = jnp.pad(b_router[0].astype(F32), (0, LANES - N_EXPERTS)).reshape(1, LANES)
    x2, x2p, top_i, gates, rank, counts = _project_norm_route(
        attn, x1, w_xo[0].astype(BF16), row(ln2_g[0]), row(ln2_b[0]), w_r, b_r)

    counts = counts[0, :N_EXPERTS].astype(jnp.int32)
    blocks_per_expert = (counts + TM_EXP - 1) // TM_EXP
    blk_end = jnp.cumsum(blocks_per_expert)
    slot_start = (blk_end - blocks_per_expert) * TM_EXP
    dest = (_pick(slot_start, top_i[:, :TOP_K]) + rank[:, :TOP_K]).reshape(T_ALL * TOP_K)
    slot_tok = jnp.zeros((N_SLOTS,), jnp.int32).at[dest].set(
        jnp.arange(T_ALL * TOP_K, dtype=jnp.int32) // TOP_K, unique_indices=True)

    xs_slots = _gather_rows(slot_tok, x2p)
    y_slots = _experts(xs_slots, blocks_per_expert, w_gu[0], b_gu[0].reshape(N_EXPERTS, 1, -1),
                       w_down[0], b_down[0].reshape(N_EXPERTS, 1, -1))

    g3, b3 = row(ln3_g[0]), row(ln3_b[0])
    n_p = T_PROMPT // TT_COMB
    y_prompt = _combine(dest, gates, x2, y_slots, g3, b3, 0, n_p)
    y_sample = _combine(dest, gates, x2, y_slots, g3, b3, n_p, T_SAMPLE // TT_COMB)
    return (y_prompt.reshape(BATCH, SEQ, D_MODEL), y_sample.reshape(1, DEC_SEQ, D_MODEL))
```

```python
import functools

import numpy as np
import jax
import jax.numpy as jnp
from jax import lax
from jax.experimental import pallas as pl
from jax.experimental.pallas import tpu as pltpu

F32 = jnp.float32
BF16 = jnp.bfloat16

D_MODEL = 2048
BATCH, SEQ = 4, 4096
DEC_SEQ = 8192
T_PROMPT = BATCH * SEQ
T_SAMPLE = DEC_SEQ
T_ALL = T_PROMPT + T_SAMPLE
GRID_W = 64
D_POOL = 1024
POOL_WINDOWS = (2, 4, 8, 16)
POOL_GROUP = 256
D_NA = 1024
NA_HEADS = 16
NA_HEAD_DIM = 64
NA_ROWS = 8
NA_COLS = 16
D_IN = D_POOL + 3 * D_NA
N_MEM = 256
XA_HEADS = 4
XA_HEAD_DIM = 512
N_EXPERTS = 32
TOP_K = 4
D_EXPERT = 2048
SWIGLU_LIMIT = 7.0
SWIGLU_ALPHA = 1.702
LN_EPS = 1e-5
DEEPNORM_ALPHA = 2.0 ** 0.25
NEG_INF = -1e30

LANES = 128
SUBLANES = 8
ROW_TILE = D_MODEL // 2 // LANES
DMA_GROUP = 8
VMEM_LIMIT = 56 * 1024 * 1024
TM_IN = 512
TM_SEQ = 512
POOL_HALO = 16
TM_ROW = 512
SUB_ROWS = 256
TM_EXP = 512
TN_GU = 512
TN_DOWN = 1024
TS_GATHER = 1024
TT_COMB = 256
HEADS_PER_GROUP = 4
N_SLOTS = T_ALL * TOP_K + N_EXPERTS * TM_EXP
N_BLOCKS = N_SLOTS // TM_EXP


def _cparams(n_axes=1):
    return pltpu.CompilerParams(dimension_semantics=("arbitrary",) * n_axes,
                                vmem_limit_bytes=VMEM_LIMIT)


def _dot(a, b):
    return jnp.dot(a, b, preferred_element_type=F32)


def _dot_nt(a, b):
    return lax.dot_general(a, b, (((1,), (1,)), ((), ())), preferred_element_type=F32)


def _layer_norm(x, g, b):
    mu = jnp.mean(x, axis=-1, keepdims=True)
    xc = x - mu
    var = jnp.mean(xc * xc, axis=-1, keepdims=True)
    return xc * lax.rsqrt(var + LN_EPS) * g + b


def _sub_tiles(n_rows):
    return [slice(r, r + SUB_ROWS) for r in range(0, n_rows, SUB_ROWS)]


def _const_spec(shape):
    nd = len(shape)
    return pl.BlockSpec(shape, lambda *_: (0,) * nd, pipeline_mode=pl.Buffered(1))


def _inproj_kernel(xp_ref, xs_ref, w_ref, o_ref, *, n_prompt_tiles):
    i = pl.program_id(0)
    x = jnp.where(i < n_prompt_tiles, xp_ref[...], xs_ref[...]).astype(BF16)
    for c in range(D_IN // 1024):
        sl = slice(c * 1024, (c + 1) * 1024)
        o_ref[:, sl] = _dot(x, w_ref[:, sl]).astype(BF16)


def _two_group_specs(tm, n_prompt_tiles):
    last = n_prompt_tiles - 1
    return (pl.BlockSpec((tm, D_MODEL), lambda i: (jnp.minimum(i, last), 0)),
            pl.BlockSpec((tm, D_MODEL), lambda i: (jnp.maximum(i - n_prompt_tiles, 0), 0)))


def _inproj(xp, xs, w_in):
    npt = T_PROMPT // TM_IN
    spec_p, spec_s = _two_group_specs(TM_IN, npt)
    return pl.pallas_call(
        functools.partial(_inproj_kernel, n_prompt_tiles=npt),
        grid=(T_ALL // TM_IN,),
        in_specs=[spec_p, spec_s, _const_spec((D_MODEL, D_IN))],
        out_specs=pl.BlockSpec((TM_IN, D_IN), lambda i: (i, 0)),
        out_shape=jax.ShapeDtypeStruct((T_ALL, D_IN), BF16),
        compiler_params=_cparams(), name="inproj")(xp, xs, w_in)


def _seq_position(i):
    per_prompt = SEQ // TM_SEQ
    n_prompt = T_PROMPT // TM_SEQ
    is_prompt = i < n_prompt
    pos = jnp.where(is_prompt, lax.rem(i, per_prompt), i - n_prompt)
    nblk = jnp.where(is_prompt, per_prompt, DEC_SEQ // TM_SEQ)
    return pos, nblk


def _pool_kernel(u_ref, prev_ref, next_ref, a_ref, wp_ref, sc_ref, o_ref):
    pos, nblk = _seq_position(pl.program_id(0))
    cur = u_ref[...]
    zero = jnp.zeros((POOL_HALO, D_POOL), BF16)
    prev = jnp.where(pos == 0, zero, prev_ref[...])
    nxt = jnp.where(pos == nblk - 1, zero, next_ref[...])
    ext = jnp.concatenate([prev, cur, nxt], axis=0)
    t = pos * TM_SEQ + lax.broadcasted_iota(jnp.int32, (TM_SEQ, 1), 0)
    seq_len = nblk * TM_SEQ
    for g, w in enumerate(POOL_WINDOWS):
        sl = slice(g * POOL_GROUP, (g + 1) * POOL_GROUP)
        win_sum = _dot(a_ref[g], ext[:, sl])
        cnt = (jnp.minimum(t - w // 2 + w, seq_len) - jnp.maximum(t - w // 2, 0)).astype(F32)
        p = win_sum / cnt - cur[:, sl].astype(F32)
        y = _dot(p.astype(BF16), wp_ref[g]) * sc_ref[:, sl]
        o_ref[:, sl] = y.astype(BF16)


def _pool_band_matrices():
    t = np.arange(TM_SEQ)[:, None]
    c = np.arange(TM_SEQ + 2 * POOL_HALO)[None, :] - POOL_HALO
    mats = [((c - t >= -(w // 2)) & (c - t <= w // 2 - 1)) for w in POOL_WINDOWS]
    return jnp.asarray(np.stack(mats).astype(np.float32), dtype=BF16)


def _pool_mixer(u, w_pool, pool_scale):
    per_halo = TM_SEQ // POOL_HALO
    n_halo = T_ALL // POOL_HALO
    return pl.pallas_call(
        _pool_kernel,
        grid=(T_ALL // TM_SEQ,),
        in_specs=[
            pl.BlockSpec((TM_SEQ, D_POOL), lambda i: (i, 0)),
            pl.BlockSpec((POOL_HALO, D_POOL), lambda i: (jnp.maximum(i * per_halo - 1, 0), 0)),
            pl.BlockSpec((POOL_HALO, D_POOL), lambda i: (jnp.minimum((i + 1) * per_halo, n_halo - 1), 0)),
            _const_spec((len(POOL_WINDOWS), TM_SEQ, TM_SEQ + 2 * POOL_HALO)),
            _const_spec((len(POOL_WINDOWS), POOL_GROUP, POOL_GROUP)),
            _const_spec((1, D_POOL)),
        ],
        out_specs=pl.BlockSpec((TM_SEQ, D_POOL), lambda i: (i, 0)),
        out_shape=jax.ShapeDtypeStruct((T_ALL, D_POOL), BF16),
        compiler_params=_cparams(), name="pool_mixer")(
            u, u, u, _pool_band_matrices(), w_pool, pool_scale)


ROWS_PER_STEP = TM_SEQ // GRID_W
KV_WINDOW_ROWS = 2 * ROWS_PER_STEP


def _na_step_geometry(s):
    steps_per_prompt = SEQ // TM_SEQ
    n_prompt_steps = T_PROMPT // TM_SEQ
    is_prompt = s < n_prompt_steps
    jj = jnp.where(is_prompt, lax.rem(s, steps_per_prompt), s - n_prompt_steps)
    rows = jnp.where(is_prompt, SEQ // GRID_W, DEC_SEQ // GRID_W)
    seq_row0 = jnp.where(is_prompt, (s // steps_per_prompt) * (SEQ // GRID_W), T_PROMPT // GRID_W)
    return jj * ROWS_PER_STEP, rows, seq_row0


def _na_window_row(s):
    r0, rows, seq_row0 = _na_step_geometry(s)
    return jnp.clip(r0 - NA_ROWS // 2, 0, rows - KV_WINDOW_ROWS), seq_row0


def _na_kernel(q_ref, k_ref, v_ref, bias_ref, o_ref):
    s = pl.program_id(0)
    r0, rows, _ = _na_step_geometry(s)
    win0, _ = _na_window_row(s)
    gl = HEADS_PER_GROUP * NA_HEAD_DIM
    lane = lax.broadcasted_iota(jnp.int32, (GRID_W, gl), 1)
    n_keys = NA_ROWS * GRID_W

    def row_body(i, carry):
        r = r0 + i
        row_start = jnp.clip(r - NA_ROWS // 2, 0, rows - NA_ROWS)
        k0 = pl.multiple_of((row_start - win0) * GRID_W, GRID_W)
        d0 = NA_ROWS - 1 + row_start - r
        q0 = pl.multiple_of(i * GRID_W, GRID_W)
        for p in range(NA_HEADS // HEADS_PER_GROUP):
            ls = slice(p * gl, (p + 1) * gl)
            q2 = q_ref[pl.ds(q0, GRID_W), ls] * jnp.asarray(NA_HEAD_DIM ** -0.5, BF16)
            zq = jnp.zeros_like(q2)
            qs = jnp.concatenate(
                [jnp.where((lane >= h * NA_HEAD_DIM) & (lane < (h + 1) * NA_HEAD_DIM), q2, zq)
                 for h in range(HEADS_PER_GROUP)], axis=0)
            kw = k_ref[pl.ds(k0, n_keys), ls]
            vw = v_ref[pl.ds(k0, n_keys), ls]
            sc = _dot_nt(qs, kw)
            sc = jnp.concatenate(
                [sc[:, j * LANES:(j + 1) * LANES] + bias_ref[p, d0 + 2 * j] for j in range(n_keys // LANES)],
                axis=1)
            m = jnp.max(sc, axis=-1, keepdims=True)
            e = jnp.exp(sc - m)
            den = jnp.sum(e, axis=-1, keepdims=True)
            o = _dot(e.astype(BF16), vw) / den
            out = o[0:GRID_W]
            for h in range(1, HEADS_PER_GROUP):
                out = jnp.where(lane >= h * NA_HEAD_DIM, o[h * GRID_W:(h + 1) * GRID_W], out)
            o_ref[pl.ds(q0, GRID_W), ls] = out.astype(BF16)
        return carry

    lax.fori_loop(0, ROWS_PER_STEP, row_body, 0, unroll=4)


def _na_bias_table(rpb):
    qc = np.arange(GRID_W)[:, None]
    kc = np.arange(GRID_W)[None, :]
    dc = np.clip(kc - qc + NA_COLS - 1, 0, 2 * NA_COLS - 2)
    col_start = np.clip(qc - NA_COLS // 2, 0, GRID_W - NA_COLS)
    mask = (kc - col_start >= 0) & (kc - col_start < NA_COLS)
    tab = jnp.where(mask[None, None], rpb[:, :, dc].astype(F32), NEG_INF)
    n_dr = 2 * NA_ROWS - 1
    ng = NA_HEADS // HEADS_PER_GROUP
    tab = tab.reshape(ng, HEADS_PER_GROUP, n_dr, GRID_W, GRID_W).transpose(0, 2, 1, 3, 4)
    tab = tab.reshape(ng, n_dr, HEADS_PER_GROUP * GRID_W, GRID_W)
    return jnp.concatenate([tab[:, :-1], tab[:, 1:]], axis=-1)


def _neighbourhood_attention(u, rpb):
    def kv_spec(col_block):
        def index(s):
            win0, seq_row0 = _na_window_row(s)
            return ((seq_row0 + win0) * GRID_W, col_block * D_NA)
        return pl.BlockSpec((pl.Element(KV_WINDOW_ROWS * GRID_W), pl.Element(D_NA)), index)

    bias = _na_bias_table(rpb)
    return pl.pallas_call(
        _na_kernel,
        grid=(T_ALL // TM_SEQ,),
        in_specs=[pl.BlockSpec((TM_SEQ, D_NA), lambda s: (s, 1)),
                  kv_spec(2), kv_spec(3), _const_spec(bias.shape)],
        out_specs=pl.BlockSpec((TM_SEQ, D_NA), lambda s: (s, 0)),
        out_shape=jax.ShapeDtypeStruct((T_ALL, D_NA), BF16),
        compiler_params=_cparams(), name="neighbourhood_attention")(u, u, u, bias)


def _mix_ln_kernel(ya_ref, yb_ref, xp_ref, xs_ref, w_ref, g_ref, b_ref, o_ref, *, n_prompt_tiles):
    i = pl.program_id(0)
    for rs in _sub_tiles(TM_ROW):
        x = jnp.where(i < n_prompt_tiles, xp_ref[rs, :], xs_ref[rs, :])
        mix = _dot(ya_ref[rs, :], w_ref[:D_POOL, :]) + _dot(yb_ref[rs, :], w_ref[D_POOL:, :])
        o_ref[rs, :] = _layer_norm(DEEPNORM_ALPHA * x + mix, g_ref[...], b_ref[...])


def _mix_ln(ya, yb, xp, xs, w_out, g, b):
    npt = T_PROMPT // TM_ROW
    spec_p, spec_s = _two_group_specs(TM_ROW, npt)
    return pl.pallas_call(
        functools.partial(_mix_ln_kernel, n_prompt_tiles=npt),
        grid=(T_ALL // TM_ROW,),
        in_specs=[pl.BlockSpec((TM_ROW, D_POOL), lambda i: (i, 0)),
                  pl.BlockSpec((TM_ROW, D_NA), lambda i: (i, 0)),
                  spec_p, spec_s,
                  _const_spec((D_MODEL, D_MODEL)), _const_spec((1, D_MODEL)), _const_spec((1, D_MODEL))],
        out_specs=pl.BlockSpec((TM_ROW, D_MODEL), lambda i: (i, 0)),
        out_shape=jax.ShapeDtypeStruct((T_ALL, D_MODEL), F32),
        compiler_params=_cparams(), name="mix_ln1")(ya, yb, xp, xs, w_out, g, b)


def _kv_kernel(m_ref, w_ref, o_ref):
    o_ref[...] = _dot(m_ref[...], w_ref[...]).astype(BF16)


def _memory_kv(mem, w_xkv):
    n_mem_rows = mem.shape[0]
    tn = 1024
    return pl.pallas_call(
        _kv_kernel,
        grid=(2 * D_MODEL // tn,),
        in_specs=[_const_spec((n_mem_rows, D_MODEL)), pl.BlockSpec((D_MODEL, tn), lambda j: (0, j))],
        out_specs=pl.BlockSpec((n_mem_rows, tn), lambda j: (0, j)),
        out_shape=jax.ShapeDtypeStruct((n_mem_rows, 2 * D_MODEL), BF16),
        compiler_params=_cparams(), name="memory_kv")(mem, w_xkv)


def _xattn_kernel(x_ref, wq_ref, k_ref, v_ref, o_ref):
    for rs in _sub_tiles(TM_ROW):
        q = _dot(x_ref[rs, :].astype(BF16), wq_ref[...]).astype(BF16)
        for h in range(XA_HEADS):
            sl = slice(h * XA_HEAD_DIM, (h + 1) * XA_HEAD_DIM)
            sc = _dot_nt(q[:, sl], k_ref[:, sl]) * (XA_HEAD_DIM ** -0.5)
            m = jnp.max(sc, axis=-1, keepdims=True)
            e = jnp.exp(sc - m)
            den = jnp.sum(e, axis=-1, keepdims=True)
            o_ref[rs, sl] = (_dot(e.astype(BF16), v_ref[:, sl]) / den).astype(BF16)


def _mem_batch(i):
    return jnp.minimum((i * TM_ROW) // SEQ, BATCH)


def _cross_attention(x1, w_xq, kv):
    return pl.pallas_call(
        _xattn_kernel,
        grid=(T_ALL // TM_ROW,),
        in_specs=[pl.BlockSpec((TM_ROW, D_MODEL), lambda i: (i, 0)),
                  _const_spec((D_MODEL, D_MODEL)),
                  pl.BlockSpec((N_MEM, D_MODEL), lambda i: (_mem_batch(i), 0)),
                  pl.BlockSpec((N_MEM, D_MODEL), lambda i: (_mem_batch(i), 1))],
        out_specs=pl.BlockSpec((TM_ROW, D_MODEL), lambda i: (i, 0)),
        out_shape=jax.ShapeDtypeStruct((T_ALL, D_MODEL), BF16),
        compiler_params=_cparams(), name="cross_attention")(x1, w_xq, kv, kv)


def _route_kernel(a_ref, x1_ref, wo_ref, g_ref, b_ref, wr_ref, br_ref,
                  x2_ref, x2p_ref, idx_ref, gate_ref, rank_ref, cnt_ref, carry_ref):
    @pl.when(pl.program_id(0) == 0)
    def _():
        carry_ref[...] = jnp.zeros_like(carry_ref)

    for rs in _sub_tiles(TM_ROW):
        _route_sub_tile(rs, a_ref, x1_ref, wo_ref, g_ref, b_ref, wr_ref, br_ref,
                        x2_ref, x2p_ref, idx_ref, gate_ref, rank_ref, carry_ref)
    cnt_ref[...] = carry_ref[...]


def _route_sub_tile(rs, a_ref, x1_ref, wo_ref, g_ref, b_ref, wr_ref, br_ref,
                    x2_ref, x2p_ref, idx_ref, gate_ref, rank_ref, carry_ref):
    n = SUB_ROWS
    x2 = _layer_norm(DEEPNORM_ALPHA * x1_ref[rs, :] + _dot(a_ref[rs, :], wo_ref[...]), g_ref[...], b_ref[...])
    x2_ref[rs, :] = x2
    half = D_MODEL // 2
    packed = pltpu.pack_elementwise([x2[:, :half], x2[:, half:]], packed_dtype=BF16)
    for s in range(ROW_TILE):
        x2p_ref[pl.ds(rs.start * ROW_TILE + s, n, stride=ROW_TILE), :] = packed[:, s * LANES:(s + 1) * LANES]

    lane = lax.broadcasted_iota(jnp.int32, (n, LANES), 1)
    lanef = lane.astype(F32)
    logits = jnp.where(lane < N_EXPERTS, _dot(x2.astype(BF16), wr_ref[...]) + br_ref[...], -jnp.inf)
    top_v, top_i, hots = [], [], []
    for _ in range(TOP_K):
        m = jnp.max(logits, axis=-1, keepdims=True)
        first = jnp.min(jnp.where(logits == m, lanef, float(LANES)), axis=-1, keepdims=True)
        hot = lanef == first
        top_v.append(m)
        top_i.append(first)
        hots.append(hot)
        logits = jnp.where(hot, -jnp.inf, logits)
    ex = [jnp.exp(v - top_v[0]) for v in top_v]
    den = ex[0] + ex[1] + ex[2] + ex[3]

    chosen = jnp.zeros((n, LANES), F32)
    for hot in hots:
        chosen = jnp.where(hot, 1.0, chosen)
    r_i = lax.broadcasted_iota(jnp.int32, (n, n), 0)
    c_i = lax.broadcasted_iota(jnp.int32, (n, n), 1)
    before = jnp.where(c_i < r_i, 1.0, 0.0).astype(BF16)
    base = carry_ref[...] + _dot(before, chosen.astype(BF16))

    idx_out = jnp.zeros((n, LANES), jnp.int32)
    gate_out = jnp.zeros((n, LANES), F32)
    rank_out = jnp.zeros((n, LANES), jnp.int32)
    for k in range(TOP_K):
        rank_k = jnp.sum(jnp.where(hots[k], base, 0.0), axis=-1, keepdims=True)
        idx_out = jnp.where(lane == k, top_i[k].astype(jnp.int32), idx_out)
        gate_out = jnp.where(lane == k, ex[k] / den, gate_out)
        rank_out = jnp.where(lane == k, rank_k.astype(jnp.int32), rank_out)
    idx_ref[rs, :] = idx_out
    gate_ref[rs, :] = gate_out
    rank_ref[rs, :] = rank_out
    carry_ref[...] = carry_ref[...] + jnp.sum(chosen, axis=0, keepdims=True)


def _project_norm_route(attn, x1, w_xo, g, b, w_router, b_router):
    row = lambda w: pl.BlockSpec((TM_ROW, w), lambda i: (i, 0))
    return pl.pallas_call(
        _route_kernel,
        grid=(T_ALL // TM_ROW,),
        in_specs=[row(D_MODEL), row(D_MODEL), _const_spec((D_MODEL, D_MODEL)),
                  _const_spec((1, D_MODEL)), _const_spec((1, D_MODEL)),
                  _const_spec((D_MODEL, LANES)), _const_spec((1, LANES))],
        out_specs=[row(D_MODEL), pl.BlockSpec((TM_ROW * ROW_TILE, LANES), lambda i: (i, 0)),
                   row(LANES), row(LANES), row(LANES),
                   pl.BlockSpec((1, LANES), lambda i: (0, 0))],
        out_shape=[jax.ShapeDtypeStruct((T_ALL, D_MODEL), F32),
                   jax.ShapeDtypeStruct((T_ALL * ROW_TILE, LANES), jnp.uint32),
                   jax.ShapeDtypeStruct((T_ALL, LANES), jnp.int32),
                   jax.ShapeDtypeStruct((T_ALL, LANES), F32),
                   jax.ShapeDtypeStruct((T_ALL, LANES), jnp.int32),
                   jax.ShapeDtypeStruct((1, LANES), F32)],
        scratch_shapes=[pltpu.VMEM((1, LANES), F32)],
        compiler_params=_cparams(), name="proj_ln2_route")(attn, x1, w_xo, g, b, w_router, b_router)


def _gather_kernel(tok_ref, x2p_ref, o_ref, sem):
    def issue(q, carry):
        for r in range(DMA_GROUP):
            j = q * DMA_GROUP + r
            src = pl.multiple_of(tok_ref[j] * ROW_TILE, ROW_TILE)
            dst = pl.multiple_of(j * ROW_TILE, ROW_TILE)
            pltpu.make_async_copy(x2p_ref.at[pl.ds(src, ROW_TILE)], o_ref.at[pl.ds(dst, ROW_TILE)],
                                  sem).start(priority=r % 2)
        return carry

    lax.fori_loop(0, TS_GATHER // DMA_GROUP, issue, 0)
    pltpu.make_async_copy(x2p_ref.at[pl.ds(0, TS_GATHER * ROW_TILE)], o_ref, sem).wait()


def _gather_rows(slot_tok, x2p):
    return pl.pallas_call(
        _gather_kernel,
        grid=(N_SLOTS // TS_GATHER,),
        in_specs=[pl.BlockSpec((TS_GATHER,), lambda i: (i,), memory_space=pltpu.SMEM),
                  pl.BlockSpec(memory_space=pl.ANY)],
        out_specs=pl.BlockSpec((TS_GATHER * ROW_TILE, LANES), lambda i: (i, 0)),
        out_shape=jax.ShapeDtypeStruct((N_SLOTS * ROW_TILE, LANES), jnp.uint32),
        scratch_shapes=[pltpu.SemaphoreType.DMA],
        compiler_params=_cparams(), name="gather_rows")(slot_tok, x2p)


def _unpack_rows(x_ref, rs):
    lo, hi = [], []
    for s in range(ROW_TILE):
        slab = x_ref[pl.ds(rs.start * ROW_TILE + s, rs.stop - rs.start, stride=ROW_TILE), :]
        lo.append(pltpu.unpack_elementwise(slab, index=0, packed_dtype=BF16, unpacked_dtype=F32).astype(BF16))
        hi.append(pltpu.unpack_elementwise(slab, index=1, packed_dtype=BF16, unpacked_dtype=F32).astype(BF16))
    return jnp.concatenate(lo, axis=1), jnp.concatenate(hi, axis=1)


STEP_VALID = 1
STEP_NEW_TILE = 2


def _gate_up_kernel(blk_ref, exp_ref, wcol_ref, ocol_ref, flag_ref, x_ref, wg_ref, wu_ref, bg_ref, bu_ref,
                    o_ref, wg_bf_ref, wu_bf_ref):
    flags = flag_ref[pl.program_id(0)]

    @pl.when((flags & STEP_NEW_TILE) != 0)
    def _():
        wg_bf_ref[...] = wg_ref[...].astype(BF16)
        wu_bf_ref[...] = wu_ref[...].astype(BF16)

    @pl.when((flags & STEP_VALID) != 0)
    def _():
        half = D_MODEL // 2
        for rs in _sub_tiles(TM_EXP):
            x_lo, x_hi = _unpack_rows(x_ref, rs)
            gate = _dot(x_lo, wg_bf_ref[:half, :]) + _dot(x_hi, wg_bf_ref[half:, :]) + bg_ref[...]
            up = _dot(x_lo, wu_bf_ref[:half, :]) + _dot(x_hi, wu_bf_ref[half:, :]) + bu_ref[...]
            gate = jnp.minimum(gate, SWIGLU_LIMIT)
            up = jnp.clip(up, -SWIGLU_LIMIT, SWIGLU_LIMIT)
            o_ref[rs, :] = (gate * jax.nn.sigmoid(SWIGLU_ALPHA * gate) * (up + 1.0)).astype(BF16)

    @pl.when((flags & STEP_VALID) == 0)
    def _():
        o_ref[...] = jnp.zeros_like(o_ref)


def _down_kernel(blk_ref, exp_ref, wcol_ref, ocol_ref, flag_ref, a_ref, w_ref, b_ref, o_ref, w_bf_ref):
    flags = flag_ref[pl.program_id(0)]

    @pl.when((flags & STEP_NEW_TILE) != 0)
    def _():
        w_bf_ref[...] = w_ref[...].astype(BF16)

    @pl.when((flags & STEP_VALID) != 0)
    def _():
        for rs in _sub_tiles(TM_EXP):
            o_ref[rs, :] = _dot(a_ref[rs, :], w_bf_ref[...]) + b_ref[...]

    @pl.when((flags & STEP_VALID) == 0)
    def _():
        o_ref[...] = jnp.zeros_like(o_ref)


def _pick(table, index):
    hot = index[..., None] == jnp.arange(table.shape[0], dtype=index.dtype)
    return jnp.sum(jnp.where(hot, table, 0), axis=-1)


def _expert_schedule(blocks_per_expert, n_col_tiles):
    n_steps = N_BLOCKS * n_col_tiles
    blk_end = jnp.cumsum(blocks_per_expert)
    blk_start = blk_end - blocks_per_expert
    total = blk_end[-1] * n_col_tiles
    step = jnp.arange(n_steps, dtype=jnp.int32)
    valid = step < total
    s = jnp.minimum(step, total - 1)
    e = jnp.sum((s[:, None] >= (blk_end * n_col_tiles)[None, :]).astype(jnp.int32), axis=1)
    start = _pick(blk_start, e)
    nb = _pick(blocks_per_expert, e)
    local = s - start * n_col_tiles
    col = local // nb
    blk = start + local - col * nb
    tile_id = e * n_col_tiles + col
    new_tile = valid & jnp.concatenate([jnp.ones((1,), bool), tile_id[1:] != tile_id[:-1]])
    flags = valid.astype(jnp.int32) * STEP_VALID + new_tile.astype(jnp.int32) * STEP_NEW_TILE
    spare = step - total
    blk = jnp.where(valid, blk, blk_end[-1] + spare // n_col_tiles)
    out_col = jnp.where(valid, col, spare % n_col_tiles)
    i32 = lambda a: a.astype(jnp.int32)
    return i32(blk), i32(e), i32(col), i32(out_col), flags


def _experts(xs, blocks_per_expert, w_gu, b_gu, w_down, b_down):
    rows = lambda s, blk, ex, wc, oc, fl: (blk[s], 0)
    out = lambda s, blk, ex, wc, oc, fl: (blk[s], oc[s])

    def weight(col_offset):
        return lambda s, blk, ex, wc, oc, fl: (ex[s], 0, col_offset + wc[s])

    n_gu = D_EXPERT // TN_GU
    sched = _expert_schedule(blocks_per_expert, n_gu)
    act = pl.pallas_call(
        _gate_up_kernel,
        grid_spec=pltpu.PrefetchScalarGridSpec(
            num_scalar_prefetch=5, grid=(N_BLOCKS * n_gu,),
            in_specs=[
                pl.BlockSpec((TM_EXP * ROW_TILE, LANES), rows),
                pl.BlockSpec((None, D_MODEL, TN_GU), weight(0)),
                pl.BlockSpec((None, D_MODEL, TN_GU), weight(n_gu)),
                pl.BlockSpec((None, 1, TN_GU), weight(0)),
                pl.BlockSpec((None, 1, TN_GU), weight(n_gu)),
            ],
            out_specs=pl.BlockSpec((TM_EXP, TN_GU), out),
            scratch_shapes=[pltpu.VMEM((D_MODEL, TN_GU), BF16), pltpu.VMEM((D_MODEL, TN_GU), BF16)]),
        out_shape=jax.ShapeDtypeStruct((N_SLOTS, D_EXPERT), BF16),
        compiler_params=_cparams(), name="expert_gate_up")(*sched, xs, w_gu, w_gu, b_gu, b_gu)

    n_dn = D_MODEL // TN_DOWN
    sched = _expert_schedule(blocks_per_expert, n_dn)
    return pl.pallas_call(
        _down_kernel,
        grid_spec=pltpu.PrefetchScalarGridSpec(
            num_scalar_prefetch=5, grid=(N_BLOCKS * n_dn,),
            in_specs=[
                pl.BlockSpec((TM_EXP, D_EXPERT), rows),
                pl.BlockSpec((None, D_EXPERT, TN_DOWN), weight(0)),
                pl.BlockSpec((None, 1, TN_DOWN), weight(0)),
            ],
            out_specs=pl.BlockSpec((TM_EXP, TN_DOWN), out),
            scratch_shapes=[pltpu.VMEM((D_EXPERT, TN_DOWN), BF16)]),
        out_shape=jax.ShapeDtypeStruct((N_SLOTS, D_MODEL), F32),
        compiler_params=_cparams(), name="expert_down")(*sched, act, w_down, b_down)


def _combine_kernel(dest_ref, gate_ref, x2_ref, y_ref, g_ref, b_ref, o_ref, buf_ref, sem):
    def issue(q, carry):
        for r in range(SUBLANES):
            t = q * SUBLANES + r
            for k in range(TOP_K):
                pltpu.make_async_copy(y_ref.at[pl.ds(dest_ref[t * TOP_K + k], 1)],
                                      buf_ref.at[k, q, pl.ds(r, 1)], sem).start(priority=k % 2)
        return carry

    n_tiles = TT_COMB // SUBLANES
    lax.fori_loop(0, n_tiles, issue, 0)
    for k in range(TOP_K):
        for q in range(n_tiles):
            pltpu.make_async_copy(y_ref.at[pl.ds(0, SUBLANES)], buf_ref.at[k, q], sem).wait()
    gates = gate_ref[...]
    plane = lambda k: buf_ref[k].reshape(TT_COMB, D_MODEL)
    h = gates[:, 0:1] * plane(0)
    for k in range(1, TOP_K):
        h = h + gates[:, k:k + 1] * plane(k)
    o_ref[...] = _layer_norm(DEEPNORM_ALPHA * x2_ref[...] + h, g_ref[...], b_ref[...])


def _combine(dest_flat, gates, x2, y_slots, g, b, first_tile, n_tiles):
    n = TT_COMB * TOP_K
    return pl.pallas_call(
        _combine_kernel,
        grid=(n_tiles,),
        in_specs=[pl.BlockSpec((n,), lambda i: (i + first_tile,), memory_space=pltpu.SMEM),
                  pl.BlockSpec((TT_COMB, LANES), lambda i: (i + first_tile, 0)),
                  pl.BlockSpec((TT_COMB, D_MODEL), lambda i: (i + first_tile, 0)),
                  pl.BlockSpec(memory_space=pl.ANY),
                  _const_spec((1, D_MODEL)), _const_spec((1, D_MODEL))],
        out_specs=pl.BlockSpec((TT_COMB, D_MODEL), lambda i: (i, 0)),
        out_shape=jax.ShapeDtypeStruct((n_tiles * TT_COMB, D_MODEL), F32),
        scratch_shapes=[pltpu.VMEM((TOP_K, TT_COMB // SUBLANES, SUBLANES, D_MODEL), F32),
                        pltpu.SemaphoreType.DMA],
        compiler_params=_cparams(), name="combine_ln3")(dest_flat, gates, x2, y_slots, g, b)


def _slot_tokens(top_i, counts, blocks_per_expert):
    n_assign = T_ALL * TOP_K
    flat = jnp.arange(n_assign, dtype=jnp.int32)
    _, order = lax.sort((top_i.reshape(n_assign), flat), num_keys=1, is_stable=True)
    sorted_tok = order // TOP_K
    blk_end = jnp.cumsum(blocks_per_expert)
    group_start = jnp.cumsum(counts) - counts
    blk = jnp.arange(N_BLOCKS, dtype=jnp.int32)
    e_blk = jnp.minimum(jnp.sum((blk[:, None] >= blk_end[None, :]).astype(jnp.int32), axis=1), N_EXPERTS - 1)
    first_blk = _pick(blk_end - blocks_per_expert, e_blk)
    per_slot = lambda v: jnp.repeat(v, TM_EXP)
    pos = per_slot((blk - first_blk) * TM_EXP) + jnp.tile(jnp.arange(TM_EXP, dtype=jnp.int32), N_BLOCKS)
    valid = per_slot(blk < blk_end[-1]) & (pos < per_slot(_pick(counts, e_blk)))
    src = jnp.clip(per_slot(_pick(group_start, e_blk)) + pos, 0, n_assign - 1)
    spread = jnp.arange(N_SLOTS, dtype=jnp.int32) % T_ALL
    return jnp.where(valid, sorted_tok[src], spread)


def kernel(x_prompt, x_sample, mem_prompt, mem_sample, w_in, w_pool, pool_scale, rpb, w_out,
           ln1_g, ln1_b, w_xq, w_xkv, w_xo, ln2_g, ln2_b,
           w_router, b_router, w_gu, b_gu, w_down, b_down, ln3_g, ln3_b):
    assert w_in.shape[0] == 1, "single-layer problem"
    xp = x_prompt.reshape(T_PROMPT, D_MODEL)
    xs = x_sample.reshape(T_SAMPLE, D_MODEL)
    mem = jnp.concatenate([mem_prompt.reshape(BATCH * N_MEM, D_MODEL),
                           mem_sample.reshape(N_MEM, D_MODEL)], axis=0).astype(BF16)
    row = lambda v: v.reshape(1, -1).astype(F32)

    u = _inproj(xp, xs, w_in[0].astype(BF16))
    ya = _pool_mixer(u, w_pool[0].astype(BF16), row(pool_scale[0]))
    yb = _neighbourhood_attention(u, rpb[0])
    x1 = _mix_ln(ya, yb, xp, xs, w_out[0].astype(BF16), row(ln1_g[0]), row(ln1_b[0]))

    kv = _memory_kv(mem, w_xkv[0].astype(BF16))
    attn = _cross_attention(x1, w_xq[0].astype(BF16), kv)
    w_r = jnp.pad(w_router[0], ((0, 0), (0, LANES - N_EXPERTS))).astype(BF16)
    b_r = jnp.pad(b_router[0].astype(F32), (0, LANES - N_EXPERTS)).reshape(1, LANES)
    x2, x2p, top_i, gates, rank, counts = _project_norm_route(
        attn, x1, w_xo[0].astype(BF16), row(ln2_g[0]), row(ln2_b[0]), w_r, b_r)

    counts = counts[0, :N_EXPERTS].astype(jnp.int32)
    blocks_per_expert = (counts + TM_EXP - 1) // TM_EXP
    blk_end = jnp.cumsum(blocks_per_expert)
    slot_start = (blk_end - blocks_per_expert) * TM_EXP
    dest = (_pick(slot_start, top_i[:, :TOP_K]) + rank[:, :TOP_K]).reshape(T_ALL * TOP_K)
    slot_tok = _slot_tokens(top_i[:, :TOP_K], counts, blocks_per_expert)

    xs_slots = _gather_rows(slot_tok, x2p)
    y_slots = _experts(xs_slots, blocks_per_expert, w_gu[0], b_gu[0].reshape(N_EXPERTS, 1, -1),
                       w_down[0], b_down[0].reshape(N_EXPERTS, 1, -1))

    g3, b3 = row(ln3_g[0]), row(ln3_b[0])
    n_p = T_PROMPT // TT_COMB
    y_prompt = _combine(dest, gates, x2, y_slots, g3, b3, 0, n_p)
    y_sample = _combine(dest, gates, x2, y_slots, g3, b3, n_p, T_SAMPLE // TT_COMB)
    return (y_prompt.reshape(BATCH, SEQ, D_MODEL), y_sample.reshape(1, DEC_SEQ, D_MODEL))
```

```python
import functools

import numpy as np
import jax
import jax.numpy as jnp
from jax import lax
from jax.experimental import pallas as pl
from jax.experimental.pallas import tpu as pltpu

F32 = jnp.float32
BF16 = jnp.bfloat16

D_MODEL = 2048
BATCH, SEQ = 4, 4096
DEC_SEQ = 8192
T_PROMPT = BATCH * SEQ
T_SAMPLE = DEC_SEQ
T_ALL = T_PROMPT + T_SAMPLE
GRID_W = 64
D_POOL = 1024
POOL_WINDOWS = (2, 4, 8, 16)
POOL_GROUP = 256
D_NA = 1024
NA_HEADS = 16
NA_HEAD_DIM = 64
NA_ROWS = 8
NA_COLS = 16
D_IN = D_POOL + 3 * D_NA
N_MEM = 256
XA_HEADS = 4
XA_HEAD_DIM = 512
N_EXPERTS = 32
TOP_K = 4
D_EXPERT = 2048
SWIGLU_LIMIT = 7.0
SWIGLU_ALPHA = 1.702
LN_EPS = 1e-5
DEEPNORM_ALPHA = 2.0 ** 0.25
NEG_INF = -1e30

LANES = 128
SUBLANES = 8
ROW_TILE = D_MODEL // 2 // LANES
DMA_GROUP = 8
VMEM_LIMIT = 56 * 1024 * 1024
TM_IN = 512
TM_SEQ = 512
POOL_HALO = 16
TM_ROW = 512
SUB_ROWS = 256
TM_EXP = 512
TN_GU = 512
TN_DOWN = 1024
TS_GATHER = 1024
TT_COMB = 256
HEADS_PER_GROUP = 4
N_SLOTS = T_ALL * TOP_K + N_EXPERTS * TM_EXP
N_BLOCKS = N_SLOTS // TM_EXP


def _cparams(n_axes=1):
    return pltpu.CompilerParams(dimension_semantics=("arbitrary",) * n_axes,
                                vmem_limit_bytes=VMEM_LIMIT)


def _dot(a, b):
    return jnp.dot(a, b, preferred_element_type=F32)


def _dot_nt(a, b):
    return lax.dot_general(a, b, (((1,), (1,)), ((), ())), preferred_element_type=F32)


def _layer_norm(x, g, b):
    mu = jnp.mean(x, axis=-1, keepdims=True)
    xc = x - mu
    var = jnp.mean(xc * xc, axis=-1, keepdims=True)
    return xc * lax.rsqrt(var + LN_EPS) * g + b


def _sub_tiles(n_rows):
    return [slice(r, r + SUB_ROWS) for r in range(0, n_rows, SUB_ROWS)]


def _const_spec(shape):
    nd = len(shape)
    return pl.BlockSpec(shape, lambda *_: (0,) * nd, pipeline_mode=pl.Buffered(1))


def _inproj_kernel(xp_ref, xs_ref, w_ref, o_ref, *, n_prompt_tiles):
    i = pl.program_id(0)
    x = jnp.where(i < n_prompt_tiles, xp_ref[...], xs_ref[...]).astype(BF16)
    for c in range(D_IN // 1024):
        sl = slice(c * 1024, (c + 1) * 1024)
        o_ref[:, sl] = _dot(x, w_ref[:, sl]).astype(BF16)


def _two_group_specs(tm, n_prompt_tiles):
    last = n_prompt_tiles - 1
    return (pl.BlockSpec((tm, D_MODEL), lambda i: (jnp.minimum(i, last), 0)),
            pl.BlockSpec((tm, D_MODEL), lambda i: (jnp.maximum(i - n_prompt_tiles, 0), 0)))


def _inproj(xp, xs, w_in):
    npt = T_PROMPT // TM_IN
    spec_p, spec_s = _two_group_specs(TM_IN, npt)
    return pl.pallas_call(
        functools.partial(_inproj_kernel, n_prompt_tiles=npt),
        grid=(T_ALL // TM_IN,),
        in_specs=[spec_p, spec_s, _const_spec((D_MODEL, D_IN))],
        out_specs=pl.BlockSpec((TM_IN, D_IN), lambda i: (i, 0)),
        out_shape=jax.ShapeDtypeStruct((T_ALL, D_IN), BF16),
        compiler_params=_cparams(), name="inproj")(xp, xs, w_in)


def _seq_position(i):
    per_prompt = SEQ // TM_SEQ
    n_prompt = T_PROMPT // TM_SEQ
    is_prompt = i < n_prompt
    pos = jnp.where(is_prompt, lax.rem(i, per_prompt), i - n_prompt)
    nblk = jnp.where(is_prompt, per_prompt, DEC_SEQ // TM_SEQ)
    return pos, nblk


def _pool_kernel(u_ref, prev_ref, next_ref, a_ref, wp_ref, sc_ref, o_ref):
    pos, nblk = _seq_position(pl.program_id(0))
    cur = u_ref[...]
    zero = jnp.zeros((POOL_HALO, D_POOL), BF16)
    prev = jnp.where(pos == 0, zero, prev_ref[...])
    nxt = jnp.where(pos == nblk - 1, zero, next_ref[...])
    ext = jnp.concatenate([prev, cur, nxt], axis=0)
    t = pos * TM_SEQ + lax.broadcasted_iota(jnp.int32, (TM_SEQ, 1), 0)
    seq_len = nblk * TM_SEQ
    for g, w in enumerate(POOL_WINDOWS):
        sl = slice(g * POOL_GROUP, (g + 1) * POOL_GROUP)
        win_sum = _dot(a_ref[g], ext[:, sl])
        cnt = (jnp.minimum(t - w // 2 + w, seq_len) - jnp.maximum(t - w // 2, 0)).astype(F32)
        p = win_sum / cnt - cur[:, sl].astype(F32)
        y = _dot(p.astype(BF16), wp_ref[g]) * sc_ref[:, sl]
        o_ref[:, sl] = y.astype(BF16)


def _pool_band_matrices():
    t = np.arange(TM_SEQ)[:, None]
    c = np.arange(TM_SEQ + 2 * POOL_HALO)[None, :] - POOL_HALO
    mats = [((c - t >= -(w // 2)) & (c - t <= w // 2 - 1)) for w in POOL_WINDOWS]
    return jnp.asarray(np.stack(mats).astype(np.float32), dtype=BF16)


def _pool_mixer(u, w_pool, pool_scale):
    per_halo = TM_SEQ // POOL_HALO
    n_halo = T_ALL // POOL_HALO
    return pl.pallas_call(
        _pool_kernel,
        grid=(T_ALL // TM_SEQ,),
        in_specs=[
            pl.BlockSpec((TM_SEQ, D_POOL), lambda i: (i, 0)),
            pl.BlockSpec((POOL_HALO, D_POOL), lambda i: (jnp.maximum(i * per_halo - 1, 0), 0)),
            pl.BlockSpec((POOL_HALO, D_POOL), lambda i: (jnp.minimum((i + 1) * per_halo, n_halo - 1), 0)),
            _const_spec((len(POOL_WINDOWS), TM_SEQ, TM_SEQ + 2 * POOL_HALO)),
            _const_spec((len(POOL_WINDOWS), POOL_GROUP, POOL_GROUP)),
            _const_spec((1, D_POOL)),
        ],
        out_specs=pl.BlockSpec((TM_SEQ, D_POOL), lambda i: (i, 0)),
        out_shape=jax.ShapeDtypeStruct((T_ALL, D_POOL), BF16),
        compiler_params=_cparams(), name="pool_mixer")(
            u, u, u, _pool_band_matrices(), w_pool, pool_scale)


ROWS_PER_STEP = TM_SEQ // GRID_W
KV_WINDOW_ROWS = 2 * ROWS_PER_STEP


def _na_step_geometry(s):
    steps_per_prompt = SEQ // TM_SEQ
    n_prompt_steps = T_PROMPT // TM_SEQ
    is_prompt = s < n_prompt_steps
    jj = jnp.where(is_prompt, lax.rem(s, steps_per_prompt), s - n_prompt_steps)
    rows = jnp.where(is_prompt, SEQ // GRID_W, DEC_SEQ // GRID_W)
    seq_row0 = jnp.where(is_prompt, (s // steps_per_prompt) * (SEQ // GRID_W), T_PROMPT // GRID_W)
    return jj * ROWS_PER_STEP, rows, seq_row0


def _na_window_row(s):
    r0, rows, seq_row0 = _na_step_geometry(s)
    return jnp.clip(r0 - NA_ROWS // 2, 0, rows - KV_WINDOW_ROWS), seq_row0


def _na_kernel(q_ref, k_ref, v_ref, bias_ref, o_ref):
    s = pl.program_id(0)
    r0, rows, _ = _na_step_geometry(s)
    win0, _ = _na_window_row(s)
    gl = HEADS_PER_GROUP * NA_HEAD_DIM
    lane = lax.broadcasted_iota(jnp.int32, (GRID_W, gl), 1)
    n_keys = NA_ROWS * GRID_W

    def row_body(i, carry):
        r = r0 + i
        row_start = jnp.clip(r - NA_ROWS // 2, 0, rows - NA_ROWS)
        k0 = pl.multiple_of((row_start - win0) * GRID_W, GRID_W)
        d0 = NA_ROWS - 1 + row_start - r
        q0 = pl.multiple_of(i * GRID_W, GRID_W)
        for p in range(NA_HEADS // HEADS_PER_GROUP):
            ls = slice(p * gl, (p + 1) * gl)
            q2 = q_ref[pl.ds(q0, GRID_W), ls] * jnp.asarray(NA_HEAD_DIM ** -0.5, BF16)
            zq = jnp.zeros_like(q2)
            qs = jnp.concatenate(
                [jnp.where((lane >= h * NA_HEAD_DIM) & (lane < (h + 1) * NA_HEAD_DIM), q2, zq)
                 for h in range(HEADS_PER_GROUP)], axis=0)
            kw = k_ref[pl.ds(k0, n_keys), ls]
            vw = v_ref[pl.ds(k0, n_keys), ls]
            sc = _dot_nt(qs, kw)
            sc = jnp.concatenate(
                [sc[:, j * LANES:(j + 1) * LANES] + bias_ref[p, d0 + 2 * j] for j in range(n_keys // LANES)],
                axis=1)
            m = jnp.max(sc, axis=-1, keepdims=True)
            e = jnp.exp(sc - m)
            den = jnp.sum(e, axis=-1, keepdims=True)
            o = _dot(e.astype(BF16), vw) / den
            out = o[0:GRID_W]
            for h in range(1, HEADS_PER_GROUP):
                out = jnp.where(lane >= h * NA_HEAD_DIM, o[h * GRID_W:(h + 1) * GRID_W], out)
            o_ref[pl.ds(q0, GRID_W), ls] = out.astype(BF16)
        return carry

    lax.fori_loop(0, ROWS_PER_STEP, row_body, 0, unroll=4)


def _na_bias_table(rpb):
    qc = np.arange(GRID_W)[:, None]
    kc = np.arange(GRID_W)[None, :]
    dc = np.clip(kc - qc + NA_COLS - 1, 0, 2 * NA_COLS - 2)
    col_start = np.clip(qc - NA_COLS // 2, 0, GRID_W - NA_COLS)
    mask = (kc - col_start >= 0) & (kc - col_start < NA_COLS)
    tab = jnp.where(mask[None, None], rpb[:, :, dc].astype(F32), NEG_INF)
    n_dr = 2 * NA_ROWS - 1
    ng = NA_HEADS // HEADS_PER_GROUP
    tab = tab.reshape(ng, HEADS_PER_GROUP, n_dr, GRID_W, GRID_W).transpose(0, 2, 1, 3, 4)
    tab = tab.reshape(ng, n_dr, HEADS_PER_GROUP * GRID_W, GRID_W)
    return jnp.concatenate([tab[:, :-1], tab[:, 1:]], axis=-1)


def _neighbourhood_attention(u, rpb):
    def kv_spec(col_block):
        def index(s):
            win0, seq_row0 = _na_window_row(s)
            return ((seq_row0 + win0) * GRID_W, col_block * D_NA)
        return pl.BlockSpec((pl.Element(KV_WINDOW_ROWS * GRID_W), pl.Element(D_NA)), index)

    bias = _na_bias_table(rpb)
    return pl.pallas_call(
        _na_kernel,
        grid=(T_ALL // TM_SEQ,),
        in_specs=[pl.BlockSpec((TM_SEQ, D_NA), lambda s: (s, 1)),
                  kv_spec(2), kv_spec(3), _const_spec(bias.shape)],
        out_specs=pl.BlockSpec((TM_SEQ, D_NA), lambda s: (s, 0)),
        out_shape=jax.ShapeDtypeStruct((T_ALL, D_NA), BF16),
        compiler_params=_cparams(), name="neighbourhood_attention")(u, u, u, bias)


def _mix_ln_kernel(ya_ref, yb_ref, xp_ref, xs_ref, w_ref, g_ref, b_ref, o_ref, *, n_prompt_tiles):
    i = pl.program_id(0)
    for rs in _sub_tiles(TM_ROW):
        x = jnp.where(i < n_prompt_tiles, xp_ref[rs, :], xs_ref[rs, :])
        mix = _dot(ya_ref[rs, :], w_ref[:D_POOL, :]) + _dot(yb_ref[rs, :], w_ref[D_POOL:, :])
        o_ref[rs, :] = _layer_norm(DEEPNORM_ALPHA * x + mix, g_ref[...], b_ref[...])


def _mix_ln(ya, yb, xp, xs, w_out, g, b):
    npt = T_PROMPT // TM_ROW
    spec_p, spec_s = _two_group_specs(TM_ROW, npt)
    return pl.pallas_call(
        functools.partial(_mix_ln_kernel, n_prompt_tiles=npt),
        grid=(T_ALL // TM_ROW,),
        in_specs=[pl.BlockSpec((TM_ROW, D_POOL), lambda i: (i, 0)),
                  pl.BlockSpec((TM_ROW, D_NA), lambda i: (i, 0)),
                  spec_p, spec_s,
                  _const_spec((D_MODEL, D_MODEL)), _const_spec((1, D_MODEL)), _const_spec((1, D_MODEL))],
        out_specs=pl.BlockSpec((TM_ROW, D_MODEL), lambda i: (i, 0)),
        out_shape=jax.ShapeDtypeStruct((T_ALL, D_MODEL), F32),
        compiler_params=_cparams(), name="mix_ln1")(ya, yb, xp, xs, w_out, g, b)


def _kv_kernel(m_ref, w_ref, o_ref):
    o_ref[...] = _dot(m_ref[...], w_ref[...]).astype(BF16)


def _memory_kv(mem, w_xkv):
    n_mem_rows = mem.shape[0]
    tn = 1024
    return pl.pallas_call(
        _kv_kernel,
        grid=(2 * D_MODEL // tn,),
        in_specs=[_const_spec((n_mem_rows, D_MODEL)), pl.BlockSpec((D_MODEL, tn), lambda j: (0, j))],
        out_specs=pl.BlockSpec((n_mem_rows, tn), lambda j: (0, j)),
        out_shape=jax.ShapeDtypeStruct((n_mem_rows, 2 * D_MODEL), BF16),
        compiler_params=_cparams(), name="memory_kv")(mem, w_xkv)


def _xattn_kernel(x_ref, wq_ref, k_ref, v_ref, o_ref):
    for rs in _sub_tiles(TM_ROW):
        q = _dot(x_ref[rs, :].astype(BF16), wq_ref[...]).astype(BF16)
        for h in range(XA_HEADS):
            sl = slice(h * XA_HEAD_DIM, (h + 1) * XA_HEAD_DIM)
            sc = _dot_nt(q[:, sl], k_ref[:, sl]) * (XA_HEAD_DIM ** -0.5)
            m = jnp.max(sc, axis=-1, keepdims=True)
            e = jnp.exp(sc - m)
            den = jnp.sum(e, axis=-1, keepdims=True)
            o_ref[rs, sl] = (_dot(e.astype(BF16), v_ref[:, sl]) / den).astype(BF16)


def _mem_batch(i):
    return jnp.minimum((i * TM_ROW) // SEQ, BATCH)


def _cross_attention(x1, w_xq, kv):
    return pl.pallas_call(
        _xattn_kernel,
        grid=(T_ALL // TM_ROW,),
        in_specs=[pl.BlockSpec((TM_ROW, D_MODEL), lambda i: (i, 0)),
                  _const_spec((D_MODEL, D_MODEL)),
                  pl.BlockSpec((N_MEM, D_MODEL), lambda i: (_mem_batch(i), 0)),
                  pl.BlockSpec((N_MEM, D_MODEL), lambda i: (_mem_batch(i), 1))],
        out_specs=pl.BlockSpec((TM_ROW, D_MODEL), lambda i: (i, 0)),
        out_shape=jax.ShapeDtypeStruct((T_ALL, D_MODEL), BF16),
        compiler_params=_cparams(), name="cross_attention")(x1, w_xq, kv, kv)


def _route_kernel(a_ref, x1_ref, wo_ref, g_ref, b_ref, wr_ref, br_ref,
                  x2_ref, x2p_ref, idx_ref, gate_ref, rank_ref, cnt_ref, carry_ref):
    @pl.when(pl.program_id(0) == 0)
    def _():
        carry_ref[...] = jnp.zeros_like(carry_ref)

    for rs in _sub_tiles(TM_ROW):
        _route_sub_tile(rs, a_ref, x1_ref, wo_ref, g_ref, b_ref, wr_ref, br_ref,
                        x2_ref, x2p_ref, idx_ref, gate_ref, rank_ref, carry_ref)
    cnt_ref[...] = carry_ref[...]


def _route_sub_tile(rs, a_ref, x1_ref, wo_ref, g_ref, b_ref, wr_ref, br_ref,
                    x2_ref, x2p_ref, idx_ref, gate_ref, rank_ref, carry_ref):
    n = SUB_ROWS
    x2 = _layer_norm(DEEPNORM_ALPHA * x1_ref[rs, :] + _dot(a_ref[rs, :], wo_ref[...]), g_ref[...], b_ref[...])
    x2_ref[rs, :] = x2
    half = D_MODEL // 2
    packed = pltpu.pack_elementwise([x2[:, :half], x2[:, half:]], packed_dtype=BF16)
    for s in range(ROW_TILE):
        x2p_ref[pl.ds(rs.start * ROW_TILE + s, n, stride=ROW_TILE), :] = packed[:, s * LANES:(s + 1) * LANES]

    lane = lax.broadcasted_iota(jnp.int32, (n, LANES), 1)
    lanef = lane.astype(F32)
    logits = jnp.where(lane < N_EXPERTS, _dot(x2.astype(BF16), wr_ref[...]) + br_ref[...], -jnp.inf)
    top_v, top_i, hots = [], [], []
    for _ in range(TOP_K):
        m = jnp.max(logits, axis=-1, keepdims=True)
        first = jnp.min(jnp.where(logits == m, lanef, float(LANES)), axis=-1, keepdims=True)
        hot = lanef == first
        top_v.append(m)
        top_i.append(first)
        hots.append(hot)
        logits = jnp.where(hot, -jnp.inf, logits)
    ex = [jnp.exp(v - top_v[0]) for v in top_v]
    den = ex[0] + ex[1] + ex[2] + ex[3]

    chosen = jnp.zeros((n, LANES), F32)
    for hot in hots:
        chosen = jnp.where(hot, 1.0, chosen)
    r_i = lax.broadcasted_iota(jnp.int32, (n, n), 0)
    c_i = lax.broadcasted_iota(jnp.int32, (n, n), 1)
    before = jnp.where(c_i < r_i, 1.0, 0.0).astype(BF16)
    base = carry_ref[...] + _dot(before, chosen.astype(BF16))

    idx_out = jnp.zeros((n, LANES), jnp.int32)
    gate_out = jnp.zeros((n, LANES), F32)
    rank_out = jnp.zeros((n, LANES), jnp.int32)
    for k in range(TOP_K):
        rank_k = jnp.sum(jnp.where(hots[k], base, 0.0), axis=-1, keepdims=True)
        idx_out = jnp.where(lane == k, top_i[k].astype(jnp.int32), idx_out)
        gate_out = jnp.where(lane == k, ex[k] / den, gate_out)
        rank_out = jnp.where(lane == k, rank_k.astype(jnp.int32), rank_out)
    idx_ref[rs, :] = idx_out
    gate_ref[rs, :] = gate_out
    rank_ref[rs, :] = rank_out
    carry_ref[...] = carry_ref[...] + jnp.sum(chosen, axis=0, keepdims=True)


def _project_norm_route(attn, x1, w_xo, g, b, w_router, b_router):
    row = lambda w: pl.BlockSpec((TM_ROW, w), lambda i: (i, 0))
    return pl.pallas_call(
        _route_kernel,
        grid=(T_ALL // TM_ROW,),
        in_specs=[row(D_MODEL), row(D_MODEL), _const_spec((D_MODEL, D_MODEL)),
                  _const_spec((1, D_MODEL)), _const_spec((1, D_MODEL)),
                  _const_spec((D_MODEL, LANES)), _const_spec((1, LANES))],
        out_specs=[row(D_MODEL), pl.BlockSpec((TM_ROW * ROW_TILE, LANES), lambda i: (i, 0)),
                   row(LANES), row(LANES), row(LANES),
                   pl.BlockSpec((1, LANES), lambda i: (0, 0))],
        out_shape=[jax.ShapeDtypeStruct((T_ALL, D_MODEL), F32),
                   jax.ShapeDtypeStruct((T_ALL * ROW_TILE, LANES), jnp.uint32),
                   jax.ShapeDtypeStruct((T_ALL, LANES), jnp.int32),
                   jax.ShapeDtypeStruct((T_ALL, LANES), F32),
                   jax.ShapeDtypeStruct((T_ALL, LANES), jnp.int32),
                   jax.ShapeDtypeStruct((1, LANES), F32)],
        scratch_shapes=[pltpu.VMEM((1, LANES), F32)],
        compiler_params=_cparams(), name="proj_ln2_route")(attn, x1, w_xo, g, b, w_router, b_router)


def _gather_kernel(tok_ref, x2p_ref, o_ref, sem):
    def issue(q, carry):
        for r in range(DMA_GROUP):
            j = q * DMA_GROUP + r
            src = pl.multiple_of(tok_ref[j] * ROW_TILE, ROW_TILE)
            dst = pl.multiple_of(j * ROW_TILE, ROW_TILE)
            pltpu.make_async_copy(x2p_ref.at[pl.ds(src, ROW_TILE)], o_ref.at[pl.ds(dst, ROW_TILE)],
                                  sem).start(priority=r % 2)
        return carry

    lax.fori_loop(0, TS_GATHER // DMA_GROUP, issue, 0)
    pltpu.make_async_copy(x2p_ref.at[pl.ds(0, TS_GATHER * ROW_TILE)], o_ref, sem).wait()


def _gather_rows(slot_tok, x2p):
    return pl.pallas_call(
        _gather_kernel,
        grid=(N_SLOTS // TS_GATHER,),
        in_specs=[pl.BlockSpec((TS_GATHER,), lambda i: (i,), memory_space=pltpu.SMEM),
                  pl.BlockSpec(memory_space=pl.ANY)],
        out_specs=pl.BlockSpec((TS_GATHER * ROW_TILE, LANES), lambda i: (i, 0)),
        out_shape=jax.ShapeDtypeStruct((N_SLOTS * ROW_TILE, LANES), jnp.uint32),
        scratch_shapes=[pltpu.SemaphoreType.DMA],
        compiler_params=_cparams(), name="gather_rows")(slot_tok, x2p)


def _unpack_rows(x_ref, rs):
    lo, hi = [], []
    for s in range(ROW_TILE):
        slab = x_ref[pl.ds(rs.start * ROW_TILE + s, rs.stop - rs.start, stride=ROW_TILE), :]
        lo.append(pltpu.unpack_elementwise(slab, index=0, packed_dtype=BF16, unpacked_dtype=F32).astype(BF16))
        hi.append(pltpu.unpack_elementwise(slab, index=1, packed_dtype=BF16, unpacked_dtype=F32).astype(BF16))
    return jnp.concatenate(lo, axis=1), jnp.concatenate(hi, axis=1)


def _stream_blocks(n_blocks, fetch, put, compute):
    @pl.when(n_blocks > 0)
    def _():
        fetch(0, 0).start()

    def do_block(i, slot):
        fetch(i, slot).wait()

        @pl.when(i + 1 < n_blocks)
        def _():
            fetch(i + 1, 1 - slot).start()

        @pl.when(i >= 2)
        def _():
            put(i - 2, slot).wait()

        compute(slot)
        put(i, slot).start()

    def pair(p, carry):
        do_block(2 * p, 0)

        @pl.when(2 * p + 1 < n_blocks)
        def _():
            do_block(2 * p + 1, 1)

        return carry

    lax.fori_loop(0, (n_blocks + 1) // 2, pair, 0)
    for slot in range(2):
        @pl.when(n_blocks > slot)
        def _():
            put(0, slot).wait()


def _zero_trailing_blocks(total_ref, obuf_ref, put_block, n_col_tiles):
    obuf_ref[0] = jnp.zeros(obuf_ref.shape[1:], obuf_ref.dtype)

    def body(b, carry):
        for col in range(n_col_tiles):
            put_block(b, col, 0).start()
        for col in range(n_col_tiles):
            put_block(b, col, 0).wait()
        return carry

    lax.fori_loop(total_ref[0], N_BLOCKS, body, 0)


def _gate_up_kernel(first_ref, nblk_ref, total_ref, xs_ref, wg_ref, wu_ref, bg_ref, bu_ref, act_ref,
                    wg_bf_ref, wu_bf_ref, xbuf_ref, obuf_ref, in_sem, out_sem, *, n_col_tiles):
    step = pl.program_id(0)
    expert = step // n_col_tiles
    col = lax.rem(step, n_col_tiles)
    first = first_ref[expert]
    wg_bf_ref[...] = wg_ref[...].astype(BF16)
    wu_bf_ref[...] = wu_ref[...].astype(BF16)

    def fetch(i, slot):
        row0 = pl.multiple_of((first + i) * (TM_EXP * ROW_TILE), TM_EXP * ROW_TILE)
        return pltpu.make_async_copy(xs_ref.at[pl.ds(row0, TM_EXP * ROW_TILE)], xbuf_ref.at[slot], in_sem.at[slot])

    def put_block(blk, c, slot):
        row0 = pl.multiple_of(blk * TM_EXP, TM_EXP)
        col0 = pl.multiple_of(c * TN_GU, TN_GU)
        return pltpu.make_async_copy(obuf_ref.at[slot], act_ref.at[pl.ds(row0, TM_EXP), pl.ds(col0, TN_GU)],
                                     out_sem.at[slot])

    def compute(slot):
        half = D_MODEL // 2
        for rs in _sub_tiles(TM_EXP):
            x_lo, x_hi = _unpack_rows(xbuf_ref.at[slot], rs)
            gate = _dot(x_lo, wg_bf_ref[:half, :]) + _dot(x_hi, wg_bf_ref[half:, :]) + bg_ref[...]
            up = _dot(x_lo, wu_bf_ref[:half, :]) + _dot(x_hi, wu_bf_ref[half:, :]) + bu_ref[...]
            gate = jnp.minimum(gate, SWIGLU_LIMIT)
            up = jnp.clip(up, -SWIGLU_LIMIT, SWIGLU_LIMIT)
            obuf_ref[slot, rs, :] = (gate * jax.nn.sigmoid(SWIGLU_ALPHA * gate) * (up + 1.0)).astype(BF16)

    _stream_blocks(nblk_ref[expert], fetch, lambda i, slot: put_block(first + i, col, slot), compute)

    @pl.when(step == pl.num_programs(0) - 1)
    def _():
        _zero_trailing_blocks(total_ref, obuf_ref, put_block, n_col_tiles)


def _down_kernel(first_ref, nblk_ref, total_ref, act_ref, w_ref, b_ref, y_ref,
                 w_bf_ref, abuf_ref, obuf_ref, in_sem, out_sem, *, n_col_tiles):
    step = pl.program_id(0)
    expert = step // n_col_tiles
    col = lax.rem(step, n_col_tiles)
    first = first_ref[expert]
    w_bf_ref[...] = w_ref[...].astype(BF16)

    def fetch(i, slot):
        row0 = pl.multiple_of((first + i) * TM_EXP, TM_EXP)
        return pltpu.make_async_copy(act_ref.at[pl.ds(row0, TM_EXP)], abuf_ref.at[slot], in_sem.at[slot])

    def put_block(blk, c, slot):
        row0 = pl.multiple_of(blk * TM_EXP, TM_EXP)
        col0 = pl.multiple_of(c * TN_DOWN, TN_DOWN)
        return pltpu.make_async_copy(obuf_ref.at[slot], y_ref.at[pl.ds(row0, TM_EXP), pl.ds(col0, TN_DOWN)],
                                     out_sem.at[slot])

    def compute(slot):
        for rs in _sub_tiles(TM_EXP):
            obuf_ref[slot, rs, :] = _dot(abuf_ref[slot, rs, :], w_bf_ref[...]) + b_ref[...]

    _stream_blocks(nblk_ref[expert], fetch, lambda i, slot: put_block(first + i, col, slot), compute)

    @pl.when(step == pl.num_programs(0) - 1)
    def _():
        _zero_trailing_blocks(total_ref, obuf_ref, put_block, n_col_tiles)


def _pick(table, index):
    hot = index[..., None] == jnp.arange(table.shape[0], dtype=index.dtype)
    return jnp.sum(jnp.where(hot, table, 0), axis=-1)


def _experts(xs, blocks_per_expert, w_gu, b_gu, w_down, b_down):
    blk_end = jnp.cumsum(blocks_per_expert)
    sched = ((blk_end - blocks_per_expert).astype(jnp.int32), blocks_per_expert.astype(jnp.int32),
             blk_end[-1:].astype(jnp.int32))
    hbm = pl.BlockSpec(memory_space=pl.ANY)

    def weight(n_col_tiles, col_offset=0):
        return lambda s, *_: (s // n_col_tiles, 0, col_offset + lax.rem(s, n_col_tiles))

    n_gu = D_EXPERT // TN_GU
    act = pl.pallas_call(
        functools.partial(_gate_up_kernel, n_col_tiles=n_gu),
        grid_spec=pltpu.PrefetchScalarGridSpec(
            num_scalar_prefetch=3, grid=(N_EXPERTS * n_gu,),
            in_specs=[hbm,
                      pl.BlockSpec((None, D_MODEL, TN_GU), weight(n_gu)),
                      pl.BlockSpec((None, D_MODEL, TN_GU), weight(n_gu, n_gu)),
                      pl.BlockSpec((None, 1, TN_GU), weight(n_gu)),
                      pl.BlockSpec((None, 1, TN_GU), weight(n_gu, n_gu))],
            out_specs=hbm,
            scratch_shapes=[pltpu.VMEM((D_MODEL, TN_GU), BF16), pltpu.VMEM((D_MODEL, TN_GU), BF16),
                            pltpu.VMEM((2, TM_EXP * ROW_TILE, LANES), jnp.uint32),
                            pltpu.VMEM((2, TM_EXP, TN_GU), BF16),
                            pltpu.SemaphoreType.DMA((2,)), pltpu.SemaphoreType.DMA((2,))]),
        out_shape=jax.ShapeDtypeStruct((N_SLOTS, D_EXPERT), BF16),
        compiler_params=_cparams(), name="expert_gate_up")(*sched, xs, w_gu, w_gu, b_gu, b_gu)

    n_dn = D_MODEL // TN_DOWN
    return pl.pallas_call(
        functools.partial(_down_kernel, n_col_tiles=n_dn),
        grid_spec=pltpu.PrefetchScalarGridSpec(
            num_scalar_prefetch=3, grid=(N_EXPERTS * n_dn,),
            in_specs=[hbm,
                      pl.BlockSpec((None, D_EXPERT, TN_DOWN), weight(n_dn)),
                      pl.BlockSpec((None, 1, TN_DOWN), weight(n_dn))],
            out_specs=hbm,
            scratch_shapes=[pltpu.VMEM((D_EXPERT, TN_DOWN), BF16),
                            pltpu.VMEM((2, TM_EXP, D_EXPERT), BF16),
                            pltpu.VMEM((2, TM_EXP, TN_DOWN), F32),
                            pltpu.SemaphoreType.DMA((2,)), pltpu.SemaphoreType.DMA((2,))]),
        out_shape=jax.ShapeDtypeStruct((N_SLOTS, D_MODEL), F32),
        compiler_params=_cparams(), name="expert_down")(*sched, act, w_down, b_down)


def _combine_kernel(dest_ref, gate_ref, x2_ref, y_ref, g_ref, b_ref, o_ref, buf_ref, sem):
    def issue(q, carry):
        for r in range(SUBLANES):
            t = q * SUBLANES + r
            for k in range(TOP_K):
                pltpu.make_async_copy(y_ref.at[pl.ds(dest_ref[t * TOP_K + k], 1)],
                                      buf_ref.at[k, q, pl.ds(r, 1)], sem).start(priority=k % 2)
        return carry

    n_tiles = TT_COMB // SUBLANES
    lax.fori_loop(0, n_tiles, issue, 0)
    for k in range(TOP_K):
        for q in range(n_tiles):
            pltpu.make_async_copy(y_ref.at[pl.ds(0, SUBLANES)], buf_ref.at[k, q], sem).wait()
    gates = gate_ref[...]
    plane = lambda k: buf_ref[k].reshape(TT_COMB, D_MODEL)
    h = gates[:, 0:1] * plane(0)
    for k in range(1, TOP_K):
        h = h + gates[:, k:k + 1] * plane(k)
    o_ref[...] = _layer_norm(DEEPNORM_ALPHA * x2_ref[...] + h, g_ref[...], b_ref[...])


def _combine(dest_flat, gates, x2, y_slots, g, b, first_tile, n_tiles):
    n = TT_COMB * TOP_K
    return pl.pallas_call(
        _combine_kernel,
        grid=(n_tiles,),
        in_specs=[pl.BlockSpec((n,), lambda i: (i + first_tile,), memory_space=pltpu.SMEM),
                  pl.BlockSpec((TT_COMB, LANES), lambda i: (i + first_tile, 0)),
                  pl.BlockSpec((TT_COMB, D_MODEL), lambda i: (i + first_tile, 0)),
                  pl.BlockSpec(memory_space=pl.ANY),
                  _const_spec((1, D_MODEL)), _const_spec((1, D_MODEL))],
        out_specs=pl.BlockSpec((TT_COMB, D_MODEL), lambda i: (i, 0)),
        out_shape=jax.ShapeDtypeStruct((n_tiles * TT_COMB, D_MODEL), F32),
        scratch_shapes=[pltpu.VMEM((TOP_K, TT_COMB // SUBLANES, SUBLANES, D_MODEL), F32),
                        pltpu.SemaphoreType.DMA],
        compiler_params=_cparams(), name="combine_ln3")(dest_flat, gates, x2, y_slots, g, b)


def _slot_tokens(top_i, counts, blocks_per_expert):
    n_assign = T_ALL * TOP_K
    flat = jnp.arange(n_assign, dtype=jnp.int32)
    _, order = lax.sort((top_i.reshape(n_assign), flat), num_keys=1, is_stable=True)
    sorted_tok = order // TOP_K
    blk_end = jnp.cumsum(blocks_per_expert)
    group_start = jnp.cumsum(counts) - counts
    blk = jnp.arange(N_BLOCKS, dtype=jnp.int32)
    e_blk = jnp.minimum(jnp.sum((blk[:, None] >= blk_end[None, :]).astype(jnp.int32), axis=1), N_EXPERTS - 1)
    first_blk = _pick(blk_end - blocks_per_expert, e_blk)
    per_slot = lambda v: jnp.repeat(v, TM_EXP)
    pos = per_slot((blk - first_blk) * TM_EXP) + jnp.tile(jnp.arange(TM_EXP, dtype=jnp.int32), N_BLOCKS)
    valid = per_slot(blk < blk_end[-1]) & (pos < per_slot(_pick(counts, e_blk)))
    src = jnp.clip(per_slot(_pick(group_start, e_blk)) + pos, 0, n_assign - 1)
    spread = jnp.arange(N_SLOTS, dtype=jnp.int32) % T_ALL
    return jnp.where(valid, sorted_tok[src], spread)


def kernel(x_prompt, x_sample, mem_prompt, mem_sample, w_in, w_pool, pool_scale, rpb, w_out,
           ln1_g, ln1_b, w_xq, w_xkv, w_xo, ln2_g, ln2_b,
           w_router, b_router, w_gu, b_gu, w_down, b_down, ln3_g, ln3_b):
    assert w_in.shape[0] == 1, "single-layer problem"
    xp = x_prompt.reshape(T_PROMPT, D_MODEL)
    xs = x_sample.reshape(T_SAMPLE, D_MODEL)
    mem = jnp.concatenate([mem_prompt.reshape(BATCH * N_MEM, D_MODEL),
                           mem_sample.reshape(N_MEM, D_MODEL)], axis=0).astype(BF16)
    row = lambda v: v.reshape(1, -1).astype(F32)

    u = _inproj(xp, xs, w_in[0].astype(BF16))
    ya = _pool_mixer(u, w_pool[0].astype(BF16), row(pool_scale[0]))
    yb = _neighbourhood_attention(u, rpb[0])
    x1 = _mix_ln(ya, yb, xp, xs, w_out[0].astype(BF16), row(ln1_g[0]), row(ln1_b[0]))

    kv = _memory_kv(mem, w_xkv[0].astype(BF16))
    attn = _cross_attention(x1, w_xq[0].astype(BF16), kv)
    w_r = jnp.pad(w_router[0], ((0, 0), (0, LANES - N_EXPERTS))).astype(BF16)
    b_r = jnp.pad(b_router[0].astype(F32), (0, LANES - N_EXPERTS)).reshape(1, LANES)
    x2, x2p, top_i, gates, rank, counts = _project_norm_route(
        attn, x1, w_xo[0].astype(BF16), row(ln2_g[0]), row(ln2_b[0]), w_r, b_r)

    counts = counts[0, :N_EXPERTS].astype(jnp.int32)
    blocks_per_expert = (counts + TM_EXP - 1) // TM_EXP
    blk_end = jnp.cumsum(blocks_per_expert)
    slot_start = (blk_end - blocks_per_expert) * TM_EXP
    dest = (_pick(slot_start, top_i[:, :TOP_K]) + rank[:, :TOP_K]).reshape(T_ALL * TOP_K)
    slot_tok = _slot_tokens(top_i[:, :TOP_K], counts, blocks_per_expert)

    xs_slots = _gather_rows(slot_tok, x2p)
    y_slots = _experts(xs_slots, blocks_per_expert, w_gu[0], b_gu[0].reshape(N_EXPERTS, 1, -1),
                       w_down[0], b_down[0].reshape(N_EXPERTS, 1, -1))

    g3, b3 = row(ln3_g[0]), row(ln3_b[0])
    n_p = T_PROMPT // TT_COMB
    y_prompt = _combine(dest, gates, x2, y_slots, g3, b3, 0, n_p)
    y_sample = _combine(dest, gates, x2, y_slots, g3, b3, n_p, T_SAMPLE // TT_COMB)
    return (y_prompt.reshape(BATCH, SEQ, D_MODEL), y_sample.reshape(1, DEC_SEQ, D_MODEL))
```

```python
import functools

import numpy as np
import jax
import jax.numpy as jnp
from jax import lax
from jax.experimental import pallas as pl
from jax.experimental.pallas import tpu as pltpu

F32 = jnp.float32
BF16 = jnp.bfloat16

D_MODEL = 2048
BATCH, SEQ = 4, 4096
DEC_SEQ = 8192
T_PROMPT = BATCH * SEQ
T_SAMPLE = DEC_SEQ
T_ALL = T_PROMPT + T_SAMPLE
GRID_W = 64
D_POOL = 1024
POOL_WINDOWS = (2, 4, 8, 16)
POOL_GROUP = 256
D_NA = 1024
NA_HEADS = 16
NA_HEAD_DIM = 64
NA_ROWS = 8
NA_COLS = 16
D_IN = D_POOL + 3 * D_NA
N_MEM = 256
XA_HEADS = 4
XA_HEAD_DIM = 512
N_EXPERTS = 32
TOP_K = 4
D_EXPERT = 2048
SWIGLU_LIMIT = 7.0
SWIGLU_ALPHA = 1.702
LN_EPS = 1e-5
DEEPNORM_ALPHA = 2.0 ** 0.25
NEG_INF = -1e30

LANES = 128
SUBLANES = 8
ROW_TILE = D_MODEL // 2 // LANES
DMA_GROUP = 8
VMEM_LIMIT = 56 * 1024 * 1024
TM_IN = 512
TM_SEQ = 512
POOL_HALO = 16
TM_ROW = 512
SUB_ROWS = 256
TM_EXP = 512
TN_GU = 1024
TN_DOWN = 1024
TS_GATHER = 1024
TT_COMB = 256
HEADS_PER_GROUP = 4
N_SLOTS = T_ALL * TOP_K + N_EXPERTS * TM_EXP
N_BLOCKS = N_SLOTS // TM_EXP


def _cparams(n_axes=1):
    return pltpu.CompilerParams(dimension_semantics=("arbitrary",) * n_axes,
                                vmem_limit_bytes=VMEM_LIMIT)


def _dot(a, b):
    return jnp.dot(a, b, preferred_element_type=F32)


def _dot_nt(a, b):
    return lax.dot_general(a, b, (((1,), (1,)), ((), ())), preferred_element_type=F32)


def _layer_norm(x, g, b):
    mu = jnp.mean(x, axis=-1, keepdims=True)
    xc = x - mu
    var = jnp.mean(xc * xc, axis=-1, keepdims=True)
    return xc * lax.rsqrt(var + LN_EPS) * g + b


def _sub_tiles(n_rows):
    return [slice(r, r + SUB_ROWS) for r in range(0, n_rows, SUB_ROWS)]


def _const_spec(shape):
    nd = len(shape)
    return pl.BlockSpec(shape, lambda *_: (0,) * nd, pipeline_mode=pl.Buffered(1))


def _inproj_kernel(xp_ref, xs_ref, w_ref, o_ref, *, n_prompt_tiles):
    i = pl.program_id(0)
    x = jnp.where(i < n_prompt_tiles, xp_ref[...], xs_ref[...]).astype(BF16)
    for c in range(D_IN // 1024):
        sl = slice(c * 1024, (c + 1) * 1024)
        o_ref[:, sl] = _dot(x, w_ref[:, sl]).astype(BF16)


def _two_group_specs(tm, n_prompt_tiles):
    last = n_prompt_tiles - 1
    return (pl.BlockSpec((tm, D_MODEL), lambda i: (jnp.minimum(i, last), 0)),
            pl.BlockSpec((tm, D_MODEL), lambda i: (jnp.maximum(i - n_prompt_tiles, 0), 0)))


def _inproj(xp, xs, w_in):
    npt = T_PROMPT // TM_IN
    spec_p, spec_s = _two_group_specs(TM_IN, npt)
    return pl.pallas_call(
        functools.partial(_inproj_kernel, n_prompt_tiles=npt),
        grid=(T_ALL // TM_IN,),
        in_specs=[spec_p, spec_s, _const_spec((D_MODEL, D_IN))],
        out_specs=pl.BlockSpec((TM_IN, D_IN), lambda i: (i, 0)),
        out_shape=jax.ShapeDtypeStruct((T_ALL, D_IN), BF16),
        compiler_params=_cparams(), name="inproj")(xp, xs, w_in)


def _seq_position(i):
    per_prompt = SEQ // TM_SEQ
    n_prompt = T_PROMPT // TM_SEQ
    is_prompt = i < n_prompt
    pos = jnp.where(is_prompt, lax.rem(i, per_prompt), i - n_prompt)
    nblk = jnp.where(is_prompt, per_prompt, DEC_SEQ // TM_SEQ)
    return pos, nblk


def _pool_kernel(u_ref, prev_ref, next_ref, a_ref, wp_ref, sc_ref, o_ref):
    pos, nblk = _seq_position(pl.program_id(0))
    cur = u_ref[...]
    zero = jnp.zeros((POOL_HALO, D_POOL), BF16)
    prev = jnp.where(pos == 0, zero, prev_ref[...])
    nxt = jnp.where(pos == nblk - 1, zero, next_ref[...])
    ext = jnp.concatenate([prev, cur, nxt], axis=0)
    t = pos * TM_SEQ + lax.broadcasted_iota(jnp.int32, (TM_SEQ, 1), 0)
    seq_len = nblk * TM_SEQ
    for g, w in enumerate(POOL_WINDOWS):
        sl = slice(g * POOL_GROUP, (g + 1) * POOL_GROUP)
        win_sum = _dot(a_ref[g], ext[:, sl])
        cnt = (jnp.minimum(t - w // 2 + w, seq_len) - jnp.maximum(t - w // 2, 0)).astype(F32)
        p = win_sum / cnt - cur[:, sl].astype(F32)
        y = _dot(p.astype(BF16), wp_ref[g]) * sc_ref[:, sl]
        o_ref[:, sl] = y.astype(BF16)


def _pool_band_matrices():
    t = np.arange(TM_SEQ)[:, None]
    c = np.arange(TM_SEQ + 2 * POOL_HALO)[None, :] - POOL_HALO
    mats = [((c - t >= -(w // 2)) & (c - t <= w // 2 - 1)) for w in POOL_WINDOWS]
    return jnp.asarray(np.stack(mats).astype(np.float32), dtype=BF16)


def _pool_mixer(u, w_pool, pool_scale):
    per_halo = TM_SEQ // POOL_HALO
    n_halo = T_ALL // POOL_HALO
    return pl.pallas_call(
        _pool_kernel,
        grid=(T_ALL // TM_SEQ,),
        in_specs=[
            pl.BlockSpec((TM_SEQ, D_POOL), lambda i: (i, 0)),
            pl.BlockSpec((POOL_HALO, D_POOL), lambda i: (jnp.maximum(i * per_halo - 1, 0), 0)),
            pl.BlockSpec((POOL_HALO, D_POOL), lambda i: (jnp.minimum((i + 1) * per_halo, n_halo - 1), 0)),
            _const_spec((len(POOL_WINDOWS), TM_SEQ, TM_SEQ + 2 * POOL_HALO)),
            _const_spec((len(POOL_WINDOWS), POOL_GROUP, POOL_GROUP)),
            _const_spec((1, D_POOL)),
        ],
        out_specs=pl.BlockSpec((TM_SEQ, D_POOL), lambda i: (i, 0)),
        out_shape=jax.ShapeDtypeStruct((T_ALL, D_POOL), BF16),
        compiler_params=_cparams(), name="pool_mixer")(
            u, u, u, _pool_band_matrices(), w_pool, pool_scale)


ROWS_PER_STEP = TM_SEQ // GRID_W
KV_WINDOW_ROWS = 2 * ROWS_PER_STEP


def _na_step_geometry(s):
    steps_per_prompt = SEQ // TM_SEQ
    n_prompt_steps = T_PROMPT // TM_SEQ
    is_prompt = s < n_prompt_steps
    jj = jnp.where(is_prompt, lax.rem(s, steps_per_prompt), s - n_prompt_steps)
    rows = jnp.where(is_prompt, SEQ // GRID_W, DEC_SEQ // GRID_W)
    seq_row0 = jnp.where(is_prompt, (s // steps_per_prompt) * (SEQ // GRID_W), T_PROMPT // GRID_W)
    return jj * ROWS_PER_STEP, rows, seq_row0


def _na_window_row(s):
    r0, rows, seq_row0 = _na_step_geometry(s)
    return jnp.clip(r0 - NA_ROWS // 2, 0, rows - KV_WINDOW_ROWS), seq_row0


def _na_kernel(q_ref, k_ref, v_ref, bias_ref, o_ref):
    s = pl.program_id(0)
    r0, rows, _ = _na_step_geometry(s)
    win0, _ = _na_window_row(s)
    gl = HEADS_PER_GROUP * NA_HEAD_DIM
    lane = lax.broadcasted_iota(jnp.int32, (GRID_W, gl), 1)
    n_keys = NA_ROWS * GRID_W

    def row_body(i, carry):
        r = r0 + i
        row_start = jnp.clip(r - NA_ROWS // 2, 0, rows - NA_ROWS)
        k0 = pl.multiple_of((row_start - win0) * GRID_W, GRID_W)
        d0 = NA_ROWS - 1 + row_start - r
        q0 = pl.multiple_of(i * GRID_W, GRID_W)
        for p in range(NA_HEADS // HEADS_PER_GROUP):
            ls = slice(p * gl, (p + 1) * gl)
            q2 = q_ref[pl.ds(q0, GRID_W), ls] * jnp.asarray(NA_HEAD_DIM ** -0.5, BF16)
            zq = jnp.zeros_like(q2)
            qs = jnp.concatenate(
                [jnp.where((lane >= h * NA_HEAD_DIM) & (lane < (h + 1) * NA_HEAD_DIM), q2, zq)
                 for h in range(HEADS_PER_GROUP)], axis=0)
            kw = k_ref[pl.ds(k0, n_keys), ls]
            vw = v_ref[pl.ds(k0, n_keys), ls]
            sc = _dot_nt(qs, kw)
            sc = jnp.concatenate(
                [sc[:, j * LANES:(j + 1) * LANES] + bias_ref[p, d0 + 2 * j] for j in range(n_keys // LANES)],
                axis=1)
            m = jnp.max(sc, axis=-1, keepdims=True)
            e = jnp.exp(sc - m)
            den = jnp.sum(e, axis=-1, keepdims=True)
            o = _dot(e.astype(BF16), vw) / den
            out = o[0:GRID_W]
            for h in range(1, HEADS_PER_GROUP):
                out = jnp.where(lane >= h * NA_HEAD_DIM, o[h * GRID_W:(h + 1) * GRID_W], out)
            o_ref[pl.ds(q0, GRID_W), ls] = out.astype(BF16)
        return carry

    lax.fori_loop(0, ROWS_PER_STEP, row_body, 0, unroll=4)


def _na_bias_table(rpb):
    qc = np.arange(GRID_W)[:, None]
    kc = np.arange(GRID_W)[None, :]
    dc = np.clip(kc - qc + NA_COLS - 1, 0, 2 * NA_COLS - 2)
    col_start = np.clip(qc - NA_COLS // 2, 0, GRID_W - NA_COLS)
    mask = (kc - col_start >= 0) & (kc - col_start < NA_COLS)
    tab = jnp.where(mask[None, None], rpb[:, :, dc].astype(F32), NEG_INF)
    n_dr = 2 * NA_ROWS - 1
    ng = NA_HEADS // HEADS_PER_GROUP
    tab = tab.reshape(ng, HEADS_PER_GROUP, n_dr, GRID_W, GRID_W).transpose(0, 2, 1, 3, 4)
    tab = tab.reshape(ng, n_dr, HEADS_PER_GROUP * GRID_W, GRID_W)
    return jnp.concatenate([tab[:, :-1], tab[:, 1:]], axis=-1)


def _neighbourhood_attention(u, rpb):
    def kv_spec(col_block):
        def index(s):
            win0, seq_row0 = _na_window_row(s)
            return ((seq_row0 + win0) * GRID_W, col_block * D_NA)
        return pl.BlockSpec((pl.Element(KV_WINDOW_ROWS * GRID_W), pl.Element(D_NA)), index)

    bias = _na_bias_table(rpb)
    return pl.pallas_call(
        _na_kernel,
        grid=(T_ALL // TM_SEQ,),
        in_specs=[pl.BlockSpec((TM_SEQ, D_NA), lambda s: (s, 1)),
                  kv_spec(2), kv_spec(3), _const_spec(bias.shape)],
        out_specs=pl.BlockSpec((TM_SEQ, D_NA), lambda s: (s, 0)),
        out_shape=jax.ShapeDtypeStruct((T_ALL, D_NA), BF16),
        compiler_params=_cparams(), name="neighbourhood_attention")(u, u, u, bias)


def _mix_ln_kernel(ya_ref, yb_ref, xp_ref, xs_ref, w_ref, g_ref, b_ref, o_ref, *, n_prompt_tiles):
    i = pl.program_id(0)
    for rs in _sub_tiles(TM_ROW):
        x = jnp.where(i < n_prompt_tiles, xp_ref[rs, :], xs_ref[rs, :])
        mix = _dot(ya_ref[rs, :], w_ref[:D_POOL, :]) + _dot(yb_ref[rs, :], w_ref[D_POOL:, :])
        o_ref[rs, :] = _layer_norm(DEEPNORM_ALPHA * x + mix, g_ref[...], b_ref[...])


def _mix_ln(ya, yb, xp, xs, w_out, g, b):
    npt = T_PROMPT // TM_ROW
    spec_p, spec_s = _two_group_specs(TM_ROW, npt)
    return pl.pallas_call(
        functools.partial(_mix_ln_kernel, n_prompt_tiles=npt),
        grid=(T_ALL // TM_ROW,),
        in_specs=[pl.BlockSpec((TM_ROW, D_POOL), lambda i: (i, 0)),
                  pl.BlockSpec((TM_ROW, D_NA), lambda i: (i, 0)),
                  spec_p, spec_s,
                  _const_spec((D_MODEL, D_MODEL)), _const_spec((1, D_MODEL)), _const_spec((1, D_MODEL))],
        out_specs=pl.BlockSpec((TM_ROW, D_MODEL), lambda i: (i, 0)),
        out_shape=jax.ShapeDtypeStruct((T_ALL, D_MODEL), F32),
        compiler_params=_cparams(), name="mix_ln1")(ya, yb, xp, xs, w_out, g, b)


def _kv_kernel(m_ref, w_ref, o_ref):
    o_ref[...] = _dot(m_ref[...], w_ref[...]).astype(BF16)


def _memory_kv(mem, w_xkv):
    n_mem_rows = mem.shape[0]
    tn = 1024
    return pl.pallas_call(
        _kv_kernel,
        grid=(2 * D_MODEL // tn,),
        in_specs=[_const_spec((n_mem_rows, D_MODEL)), pl.BlockSpec((D_MODEL, tn), lambda j: (0, j))],
        out_specs=pl.BlockSpec((n_mem_rows, tn), lambda j: (0, j)),
        out_shape=jax.ShapeDtypeStruct((n_mem_rows, 2 * D_MODEL), BF16),
        compiler_params=_cparams(), name="memory_kv")(mem, w_xkv)


def _xattn_kernel(x_ref, wq_ref, k_ref, v_ref, o_ref):
    for rs in _sub_tiles(TM_ROW):
        q = _dot(x_ref[rs, :].astype(BF16), wq_ref[...]).astype(BF16)
        for h in range(XA_HEADS):
            sl = slice(h * XA_HEAD_DIM, (h + 1) * XA_HEAD_DIM)
            sc = _dot_nt(q[:, sl], k_ref[:, sl]) * (XA_HEAD_DIM ** -0.5)
            m = jnp.max(sc, axis=-1, keepdims=True)
            e = jnp.exp(sc - m)
            den = jnp.sum(e, axis=-1, keepdims=True)
            o_ref[rs, sl] = (_dot(e.astype(BF16), v_ref[:, sl]) / den).astype(BF16)


def _mem_batch(i):
    return jnp.minimum((i * TM_ROW) // SEQ, BATCH)


def _cross_attention(x1, w_xq, kv):
    return pl.pallas_call(
        _xattn_kernel,
        grid=(T_ALL // TM_ROW,),
        in_specs=[pl.BlockSpec((TM_ROW, D_MODEL), lambda i: (i, 0)),
                  _const_spec((D_MODEL, D_MODEL)),
                  pl.BlockSpec((N_MEM, D_MODEL), lambda i: (_mem_batch(i), 0)),
                  pl.BlockSpec((N_MEM, D_MODEL), lambda i: (_mem_batch(i), 1))],
        out_specs=pl.BlockSpec((TM_ROW, D_MODEL), lambda i: (i, 0)),
        out_shape=jax.ShapeDtypeStruct((T_ALL, D_MODEL), BF16),
        compiler_params=_cparams(), name="cross_attention")(x1, w_xq, kv, kv)


def _route_kernel(a_ref, x1_ref, wo_ref, g_ref, b_ref, wr_ref, br_ref,
                  x2_ref, x2p_ref, idx_ref, gate_ref, rank_ref, cnt_ref, carry_ref):
    @pl.when(pl.program_id(0) == 0)
    def _():
        carry_ref[...] = jnp.zeros_like(carry_ref)

    for rs in _sub_tiles(TM_ROW):
        _route_sub_tile(rs, a_ref, x1_ref, wo_ref, g_ref, b_ref, wr_ref, br_ref,
                        x2_ref, x2p_ref, idx_ref, gate_ref, rank_ref, carry_ref)
    cnt_ref[...] = carry_ref[...]


def _route_sub_tile(rs, a_ref, x1_ref, wo_ref, g_ref, b_ref, wr_ref, br_ref,
                    x2_ref, x2p_ref, idx_ref, gate_ref, rank_ref, carry_ref):
    n = SUB_ROWS
    x2 = _layer_norm(DEEPNORM_ALPHA * x1_ref[rs, :] + _dot(a_ref[rs, :], wo_ref[...]), g_ref[...], b_ref[...])
    x2_ref[rs, :] = x2
    half = D_MODEL // 2
    packed = pltpu.pack_elementwise([x2[:, :half], x2[:, half:]], packed_dtype=BF16)
    for s in range(ROW_TILE):
        x2p_ref[pl.ds(rs.start * ROW_TILE + s, n, stride=ROW_TILE), :] = packed[:, s * LANES:(s + 1) * LANES]

    lane = lax.broadcasted_iota(jnp.int32, (n, LANES), 1)
    lanef = lane.astype(F32)
    logits = jnp.where(lane < N_EXPERTS, _dot(x2.astype(BF16), wr_ref[...]) + br_ref[...], -jnp.inf)
    top_v, top_i, hots = [], [], []
    for _ in range(TOP_K):
        m = jnp.max(logits, axis=-1, keepdims=True)
        first = jnp.min(jnp.where(logits == m, lanef, float(LANES)), axis=-1, keepdims=True)
        hot = lanef == first
        top_v.append(m)
        top_i.append(first)
        hots.append(hot)
        logits = jnp.where(hot, -jnp.inf, logits)
    ex = [jnp.exp(v - top_v[0]) for v in top_v]
    den = ex[0] + ex[1] + ex[2] + ex[3]

    chosen = jnp.zeros((n, LANES), F32)
    for hot in hots:
        chosen = jnp.where(hot, 1.0, chosen)
    r_i = lax.broadcasted_iota(jnp.int32, (n, n), 0)
    c_i = lax.broadcasted_iota(jnp.int32, (n, n), 1)
    before = jnp.where(c_i < r_i, 1.0, 0.0).astype(BF16)
    base = carry_ref[...] + _dot(before, chosen.astype(BF16))

    idx_out = jnp.zeros((n, LANES), jnp.int32)
    gate_out = jnp.zeros((n, LANES), F32)
    rank_out = jnp.zeros((n, LANES), jnp.int32)
    for k in range(TOP_K):
        rank_k = jnp.sum(jnp.where(hots[k], base, 0.0), axis=-1, keepdims=True)
        idx_out = jnp.where(lane == k, top_i[k].astype(jnp.int32), idx_out)
        gate_out = jnp.where(lane == k, ex[k] / den, gate_out)
        rank_out = jnp.where(lane == k, rank_k.astype(jnp.int32), rank_out)
    idx_ref[rs, :] = idx_out
    gate_ref[rs, :] = gate_out
    rank_ref[rs, :] = rank_out
    carry_ref[...] = carry_ref[...] + jnp.sum(chosen, axis=0, keepdims=True)


def _project_norm_route(attn, x1, w_xo, g, b, w_router, b_router):
    row = lambda w: pl.BlockSpec((TM_ROW, w), lambda i: (i, 0))
    return pl.pallas_call(
        _route_kernel,
        grid=(T_ALL // TM_ROW,),
        in_specs=[row(D_MODEL), row(D_MODEL), _const_spec((D_MODEL, D_MODEL)),
                  _const_spec((1, D_MODEL)), _const_spec((1, D_MODEL)),
                  _const_spec((D_MODEL, LANES)), _const_spec((1, LANES))],
        out_specs=[row(D_MODEL), pl.BlockSpec((TM_ROW * ROW_TILE, LANES), lambda i: (i, 0)),
                   row(LANES), row(LANES), row(LANES),
                   pl.BlockSpec((1, LANES), lambda i: (0, 0))],
        out_shape=[jax.ShapeDtypeStruct((T_ALL, D_MODEL), F32),
                   jax.ShapeDtypeStruct((T_ALL * ROW_TILE, LANES), jnp.uint32),
                   jax.ShapeDtypeStruct((T_ALL, LANES), jnp.int32),
                   jax.ShapeDtypeStruct((T_ALL, LANES), F32),
                   jax.ShapeDtypeStruct((T_ALL, LANES), jnp.int32),
                   jax.ShapeDtypeStruct((1, LANES), F32)],
        scratch_shapes=[pltpu.VMEM((1, LANES), F32)],
        compiler_params=_cparams(), name="proj_ln2_route")(attn, x1, w_xo, g, b, w_router, b_router)


def _gather_kernel(tok_ref, x2p_ref, o_ref, sem):
    def issue(q, carry):
        for r in range(DMA_GROUP):
            j = q * DMA_GROUP + r
            src = pl.multiple_of(tok_ref[j] * ROW_TILE, ROW_TILE)
            dst = pl.multiple_of(j * ROW_TILE, ROW_TILE)
            pltpu.make_async_copy(x2p_ref.at[pl.ds(src, ROW_TILE)], o_ref.at[pl.ds(dst, ROW_TILE)],
                                  sem).start(priority=r % 2)
        return carry

    lax.fori_loop(0, TS_GATHER // DMA_GROUP, issue, 0)
    pltpu.make_async_copy(x2p_ref.at[pl.ds(0, TS_GATHER * ROW_TILE)], o_ref, sem).wait()


def _gather_rows(slot_tok, x2p):
    return pl.pallas_call(
        _gather_kernel,
        grid=(N_SLOTS // TS_GATHER,),
        in_specs=[pl.BlockSpec((TS_GATHER,), lambda i: (i,), memory_space=pltpu.SMEM),
                  pl.BlockSpec(memory_space=pl.ANY)],
        out_specs=pl.BlockSpec((TS_GATHER * ROW_TILE, LANES), lambda i: (i, 0)),
        out_shape=jax.ShapeDtypeStruct((N_SLOTS * ROW_TILE, LANES), jnp.uint32),
        scratch_shapes=[pltpu.SemaphoreType.DMA],
        compiler_params=_cparams(), name="gather_rows")(slot_tok, x2p)


def _unpack_rows(x_ref, rs):
    lo, hi = [], []
    for s in range(ROW_TILE):
        slab = x_ref[pl.ds(rs.start * ROW_TILE + s, rs.stop - rs.start, stride=ROW_TILE), :]
        lo.append(pltpu.unpack_elementwise(slab, index=0, packed_dtype=BF16, unpacked_dtype=F32).astype(BF16))
        hi.append(pltpu.unpack_elementwise(slab, index=1, packed_dtype=BF16, unpacked_dtype=F32).astype(BF16))
    return jnp.concatenate(lo, axis=1), jnp.concatenate(hi, axis=1)


STEP_VALID = 1
STEP_NEW_TILE = 2


def _gate_up_kernel(blk_ref, exp_ref, wcol_ref, ocol_ref, flag_ref, x_ref, wg_ref, wu_ref, bg_ref, bu_ref,
                    o_ref, wg_bf_ref, wu_bf_ref):
    flags = flag_ref[pl.program_id(0)]

    @pl.when((flags & STEP_NEW_TILE) != 0)
    def _():
        wg_bf_ref[...] = wg_ref[...].astype(BF16)
        wu_bf_ref[...] = wu_ref[...].astype(BF16)

    @pl.when((flags & STEP_VALID) != 0)
    def _():
        half = D_MODEL // 2
        x_lo, x_hi = _unpack_rows(x_ref, slice(0, TM_EXP))
        gate = _dot(x_lo, wg_bf_ref[:half, :]) + _dot(x_hi, wg_bf_ref[half:, :]) + bg_ref[0:1, :]
        up = _dot(x_lo, wu_bf_ref[:half, :]) + _dot(x_hi, wu_bf_ref[half:, :]) + bu_ref[0:1, :]
        gate = jnp.minimum(gate, SWIGLU_LIMIT)
        up = jnp.clip(up, -SWIGLU_LIMIT, SWIGLU_LIMIT)
        o_ref[...] = (gate * jax.nn.sigmoid(SWIGLU_ALPHA * gate) * (up + 1.0)).astype(BF16)

    @pl.when((flags & STEP_VALID) == 0)
    def _():
        o_ref[...] = jnp.zeros_like(o_ref)


def _down_kernel(blk_ref, exp_ref, wcol_ref, ocol_ref, flag_ref, a_ref, w_ref, b_ref, o_ref, w_bf_ref):
    flags = flag_ref[pl.program_id(0)]

    @pl.when((flags & STEP_NEW_TILE) != 0)
    def _():
        w_bf_ref[...] = w_ref[...].astype(BF16)

    @pl.when((flags & STEP_VALID) != 0)
    def _():
        o_ref[...] = _dot(a_ref[...], w_bf_ref[...]) + b_ref[0:1, :]

    @pl.when((flags & STEP_VALID) == 0)
    def _():
        o_ref[...] = jnp.zeros_like(o_ref)


def _pick(table, index):
    hot = index[..., None] == jnp.arange(table.shape[0], dtype=index.dtype)
    return jnp.sum(jnp.where(hot, table, 0), axis=-1)


def _expert_schedule(blocks_per_expert, n_col_tiles):
    n_steps = N_BLOCKS * n_col_tiles
    blk_end = jnp.cumsum(blocks_per_expert)
    blk_start = blk_end - blocks_per_expert
    total = blk_end[-1] * n_col_tiles
    step = jnp.arange(n_steps, dtype=jnp.int32)
    valid = step < total
    s = jnp.minimum(step, total - 1)
    e = jnp.sum((s[:, None] >= (blk_end * n_col_tiles)[None, :]).astype(jnp.int32), axis=1)
    start = _pick(blk_start, e)
    nb = _pick(blocks_per_expert, e)
    local = s - start * n_col_tiles
    col = local // nb
    blk = start + local - col * nb
    tile_id = e * n_col_tiles + col
    new_tile = valid & jnp.concatenate([jnp.ones((1,), bool), tile_id[1:] != tile_id[:-1]])
    flags = valid.astype(jnp.int32) * STEP_VALID + new_tile.astype(jnp.int32) * STEP_NEW_TILE
    spare = step - total
    blk = jnp.where(valid, blk, blk_end[-1] + spare // n_col_tiles)
    out_col = jnp.where(valid, col, spare % n_col_tiles)
    i32 = lambda a: a.astype(jnp.int32)
    return i32(blk), i32(e), i32(col), i32(out_col), flags


def _experts(xs, blocks_per_expert, w_gu, b_gu, w_down, b_down):
    rows = lambda s, blk, ex, wc, oc, fl: (blk[s], 0)
    out = lambda s, blk, ex, wc, oc, fl: (blk[s], oc[s])

    def weight(col_offset):
        return lambda s, blk, ex, wc, oc, fl: (ex[s], 0, col_offset + wc[s])

    n_gu = D_EXPERT // TN_GU
    sched = _expert_schedule(blocks_per_expert, n_gu)
    act = pl.pallas_call(
        _gate_up_kernel,
        grid_spec=pltpu.PrefetchScalarGridSpec(
            num_scalar_prefetch=5, grid=(N_BLOCKS * n_gu,),
            in_specs=[
                pl.BlockSpec((TM_EXP * ROW_TILE, LANES), rows),
                pl.BlockSpec((None, D_MODEL, TN_GU), weight(0)),
                pl.BlockSpec((None, D_MODEL, TN_GU), weight(n_gu)),
                pl.BlockSpec((None, SUBLANES, TN_GU), weight(0)),
                pl.BlockSpec((None, SUBLANES, TN_GU), weight(n_gu)),
            ],
            out_specs=pl.BlockSpec((TM_EXP, TN_GU), out),
            scratch_shapes=[pltpu.VMEM((D_MODEL, TN_GU), BF16), pltpu.VMEM((D_MODEL, TN_GU), BF16)]),
        out_shape=jax.ShapeDtypeStruct((N_SLOTS, D_EXPERT), BF16),
        compiler_params=_cparams(), name="expert_gate_up")(*sched, xs, w_gu, w_gu, b_gu, b_gu)

    n_dn = D_MODEL // TN_DOWN
    sched = _expert_schedule(blocks_per_expert, n_dn)
    return pl.pallas_call(
        _down_kernel,
        grid_spec=pltpu.PrefetchScalarGridSpec(
            num_scalar_prefetch=5, grid=(N_BLOCKS * n_dn,),
            in_specs=[
                pl.BlockSpec((TM_EXP, D_EXPERT), rows),
                pl.BlockSpec((None, D_EXPERT, TN_DOWN), weight(0)),
                pl.BlockSpec((None, SUBLANES, TN_DOWN), weight(0)),
            ],
            out_specs=pl.BlockSpec((TM_EXP, TN_DOWN), out),
            scratch_shapes=[pltpu.VMEM((D_EXPERT, TN_DOWN), BF16)]),
        out_shape=jax.ShapeDtypeStruct((N_SLOTS, D_MODEL), F32),
        compiler_params=_cparams(), name="expert_down")(*sched, act, w_down, b_down)


def _combine_kernel(dest_ref, gate_ref, x2_ref, y_ref, g_ref, b_ref, o_ref, buf_ref, sem):
    def issue(q, carry):
        for r in range(SUBLANES):
            t = q * SUBLANES + r
            for k in range(TOP_K):
                pltpu.make_async_copy(y_ref.at[pl.ds(dest_ref[t * TOP_K + k], 1)],
                                      buf_ref.at[k, q, pl.ds(r, 1)], sem).start(priority=k % 2)
        return carry

    n_tiles = TT_COMB // SUBLANES
    lax.fori_loop(0, n_tiles, issue, 0)
    for k in range(TOP_K):
        for q in range(n_tiles):
            pltpu.make_async_copy(y_ref.at[pl.ds(0, SUBLANES)], buf_ref.at[k, q], sem).wait()
    gates = gate_ref[...]
    plane = lambda k: buf_ref[k].reshape(TT_COMB, D_MODEL)
    h = gates[:, 0:1] * plane(0)
    for k in range(1, TOP_K):
        h = h + gates[:, k:k + 1] * plane(k)
    o_ref[...] = _layer_norm(DEEPNORM_ALPHA * x2_ref[...] + h, g_ref[...], b_ref[...])


def _combine(dest_flat, gates, x2, y_slots, g, b, first_tile, n_tiles):
    n = TT_COMB * TOP_K
    return pl.pallas_call(
        _combine_kernel,
        grid=(n_tiles,),
        in_specs=[pl.BlockSpec((n,), lambda i: (i + first_tile,), memory_space=pltpu.SMEM),
                  pl.BlockSpec((TT_COMB, LANES), lambda i: (i + first_tile, 0)),
                  pl.BlockSpec((TT_COMB, D_MODEL), lambda i: (i + first_tile, 0)),
                  pl.BlockSpec(memory_space=pl.ANY),
                  _const_spec((1, D_MODEL)), _const_spec((1, D_MODEL))],
        out_specs=pl.BlockSpec((TT_COMB, D_MODEL), lambda i: (i, 0)),
        out_shape=jax.ShapeDtypeStruct((n_tiles * TT_COMB, D_MODEL), F32),
        scratch_shapes=[pltpu.VMEM((TOP_K, TT_COMB // SUBLANES, SUBLANES, D_MODEL), F32),
                        pltpu.SemaphoreType.DMA],
        compiler_params=_cparams(), name="combine_ln3")(dest_flat, gates, x2, y_slots, g, b)


def _slot_tokens(top_i, counts, blocks_per_expert):
    n_assign = T_ALL * TOP_K
    flat = jnp.arange(n_assign, dtype=jnp.int32)
    _, order = lax.sort((top_i.reshape(n_assign), flat), num_keys=1, is_stable=True)
    sorted_tok = order // TOP_K
    blk_end = jnp.cumsum(blocks_per_expert)
    group_start = jnp.cumsum(counts) - counts
    blk = jnp.arange(N_BLOCKS, dtype=jnp.int32)
    e_blk = jnp.minimum(jnp.sum((blk[:, None] >= blk_end[None, :]).astype(jnp.int32), axis=1), N_EXPERTS - 1)
    first_blk = _pick(blk_end - blocks_per_expert, e_blk)
    per_slot = lambda v: jnp.repeat(v, TM_EXP)
    pos = per_slot((blk - first_blk) * TM_EXP) + jnp.tile(jnp.arange(TM_EXP, dtype=jnp.int32), N_BLOCKS)
    valid = per_slot(blk < blk_end[-1]) & (pos < per_slot(_pick(counts, e_blk)))
    src = jnp.clip(per_slot(_pick(group_start, e_blk)) + pos, 0, n_assign - 1)
    spread = jnp.arange(N_SLOTS, dtype=jnp.int32) % T_ALL
    return jnp.where(valid, sorted_tok[src], spread)


def kernel(x_prompt, x_sample, mem_prompt, mem_sample, w_in, w_pool, pool_scale, rpb, w_out,
           ln1_g, ln1_b, w_xq, w_xkv, w_xo, ln2_g, ln2_b,
           w_router, b_router, w_gu, b_gu, w_down, b_down, ln3_g, ln3_b):
    assert w_in.shape[0] == 1, "single-layer problem"
    xp = x_prompt.reshape(T_PROMPT, D_MODEL)
    xs = x_sample.reshape(T_SAMPLE, D_MODEL)
    mem = jnp.concatenate([mem_prompt.reshape(BATCH * N_MEM, D_MODEL),
                           mem_sample.reshape(N_MEM, D_MODEL)], axis=0).astype(BF16)
    row = lambda v: v.reshape(1, -1).astype(F32)

    u = _inproj(xp, xs, w_in[0].astype(BF16))
    ya = _pool_mixer(u, w_pool[0].astype(BF16), row(pool_scale[0]))
    yb = _neighbourhood_attention(u, rpb[0])
    x1 = _mix_ln(ya, yb, xp, xs, w_out[0].astype(BF16), row(ln1_g[0]), row(ln1_b[0]))

    kv = _memory_kv(mem, w_xkv[0].astype(BF16))
    attn = _cross_attention(x1, w_xq[0].astype(BF16), kv)
    w_r = jnp.pad(w_router[0], ((0, 0), (0, LANES - N_EXPERTS))).astype(BF16)
    b_r = jnp.pad(b_router[0].astype(F32), (0, LANES - N_EXPERTS)).reshape(1, LANES)
    x2, x2p, top_i, gates, rank, counts = _project_norm_route(
        attn, x1, w_xo[0].astype(BF16), row(ln2_g[0]), row(ln2_b[0]), w_r, b_r)

    counts = counts[0, :N_EXPERTS].astype(jnp.int32)
    blocks_per_expert = (counts + TM_EXP - 1) // TM_EXP
    blk_end = jnp.cumsum(blocks_per_expert)
    slot_start = (blk_end - blocks_per_expert) * TM_EXP
    dest = (_pick(slot_start, top_i[:, :TOP_K]) + rank[:, :TOP_K]).reshape(T_ALL * TOP_K)
    slot_tok = _slot_tokens(top_i[:, :TOP_K], counts, blocks_per_expert)

    xs_slots = _gather_rows(slot_tok, x2p)
    tiled = lambda bias: jnp.broadcast_to(bias[:, None, :], (N_EXPERTS, SUBLANES, bias.shape[-1]))
    y_slots = _experts(xs_slots, blocks_per_expert, w_gu[0], tiled(b_gu[0]), w_down[0], tiled(b_down[0]))

    g3, b3 = row(ln3_g[0]), row(ln3_b[0])
    n_p = T_PROMPT // TT_COMB
    y_prompt = _combine(dest, gates, x2, y_slots, g3, b3, 0, n_p)
    y_sample = _combine(dest, gates, x2, y_slots, g3, b3, n_p, T_SAMPLE // TT_COMB)
    return (y_prompt.reshape(BATCH, SEQ, D_MODEL), y_sample.reshape(1, DEC_SEQ, D_MODEL))
```

```python
import functools

import numpy as np
import jax
import jax.numpy as jnp
from jax import lax
from jax.experimental import pallas as pl
from jax.experimental.pallas import tpu as pltpu

F32 = jnp.float32
BF16 = jnp.bfloat16

D_MODEL = 2048
BATCH, SEQ = 4, 4096
DEC_SEQ = 8192
T_PROMPT = BATCH * SEQ
T_SAMPLE = DEC_SEQ
T_ALL = T_PROMPT + T_SAMPLE
GRID_W = 64
D_POOL = 1024
POOL_WINDOWS = (2, 4, 8, 16)
POOL_GROUP = 256
D_NA = 1024
NA_HEADS = 16
NA_HEAD_DIM = 64
NA_ROWS = 8
NA_COLS = 16
D_IN = D_POOL + 3 * D_NA
N_MEM = 256
XA_HEADS = 4
XA_HEAD_DIM = 512
N_EXPERTS = 32
TOP_K = 4
D_EXPERT = 2048
SWIGLU_LIMIT = 7.0
SWIGLU_ALPHA = 1.702
LN_EPS = 1e-5
DEEPNORM_ALPHA = 2.0 ** 0.25
NEG_INF = -1e30

LANES = 128
SUBLANES = 8
ROW_TILE = D_MODEL // 2 // LANES
DMA_GROUP = 8
VMEM_LIMIT = 56 * 1024 * 1024
TM_IN = 512
TM_SEQ = 512
POOL_HALO = 16
TM_ROW = 512
SUB_ROWS = 256
TM_EXP = 512
TN_GU = 1024
TN_DOWN = 1024
TS_GATHER = 2048
TT_COMB = 512
HEADS_PER_GROUP = 4
N_SLOTS = T_ALL * TOP_K + N_EXPERTS * TM_EXP
N_BLOCKS = N_SLOTS // TM_EXP


def _cparams(n_axes=1):
    return pltpu.CompilerParams(dimension_semantics=("arbitrary",) * n_axes,
                                vmem_limit_bytes=VMEM_LIMIT)


def _dot(a, b):
    return jnp.dot(a, b, preferred_element_type=F32)


def _dot_nt(a, b):
    return lax.dot_general(a, b, (((1,), (1,)), ((), ())), preferred_element_type=F32)


def _layer_norm(x, g, b):
    mu = jnp.mean(x, axis=-1, keepdims=True)
    xc = x - mu
    var = jnp.mean(xc * xc, axis=-1, keepdims=True)
    return xc * lax.rsqrt(var + LN_EPS) * g + b


def _sub_tiles(n_rows):
    return [slice(r, r + SUB_ROWS) for r in range(0, n_rows, SUB_ROWS)]


def _const_spec(shape):
    nd = len(shape)
    return pl.BlockSpec(shape, lambda *_: (0,) * nd, pipeline_mode=pl.Buffered(1))


def _inproj_kernel(xp_ref, xs_ref, w_ref, o_ref, *, n_prompt_tiles):
    i = pl.program_id(0)
    x = jnp.where(i < n_prompt_tiles, xp_ref[...], xs_ref[...]).astype(BF16)
    for c in range(D_IN // 1024):
        sl = slice(c * 1024, (c + 1) * 1024)
        o_ref[:, sl] = _dot(x, w_ref[:, sl]).astype(BF16)


def _two_group_specs(tm, n_prompt_tiles):
    last = n_prompt_tiles - 1
    return (pl.BlockSpec((tm, D_MODEL), lambda i: (jnp.minimum(i, last), 0)),
            pl.BlockSpec((tm, D_MODEL), lambda i: (jnp.maximum(i - n_prompt_tiles, 0), 0)))


def _inproj(xp, xs, w_in):
    npt = T_PROMPT // TM_IN
    spec_p, spec_s = _two_group_specs(TM_IN, npt)
    return pl.pallas_call(
        functools.partial(_inproj_kernel, n_prompt_tiles=npt),
        grid=(T_ALL // TM_IN,),
        in_specs=[spec_p, spec_s, _const_spec((D_MODEL, D_IN))],
        out_specs=pl.BlockSpec((TM_IN, D_IN), lambda i: (i, 0)),
        out_shape=jax.ShapeDtypeStruct((T_ALL, D_IN), BF16),
        compiler_params=_cparams(), name="inproj")(xp, xs, w_in)


def _seq_position(i):
    per_prompt = SEQ // TM_SEQ
    n_prompt = T_PROMPT // TM_SEQ
    is_prompt = i < n_prompt
    pos = jnp.where(is_prompt, lax.rem(i, per_prompt), i - n_prompt)
    nblk = jnp.where(is_prompt, per_prompt, DEC_SEQ // TM_SEQ)
    return pos, nblk


def _pool_kernel(u_ref, prev_ref, next_ref, a_ref, wp_ref, sc_ref, o_ref):
    pos, nblk = _seq_position(pl.program_id(0))
    cur = u_ref[...]
    zero = jnp.zeros((POOL_HALO, D_POOL), BF16)
    prev = jnp.where(pos == 0, zero, prev_ref[...])
    nxt = jnp.where(pos == nblk - 1, zero, next_ref[...])
    ext = jnp.concatenate([prev, cur, nxt], axis=0)
    t = pos * TM_SEQ + lax.broadcasted_iota(jnp.int32, (TM_SEQ, 1), 0)
    seq_len = nblk * TM_SEQ
    for g, w in enumerate(POOL_WINDOWS):
        sl = slice(g * POOL_GROUP, (g + 1) * POOL_GROUP)
        win_sum = _dot(a_ref[g], ext[:, sl])
        cnt = (jnp.minimum(t - w // 2 + w, seq_len) - jnp.maximum(t - w // 2, 0)).astype(F32)
        p = win_sum / cnt - cur[:, sl].astype(F32)
        y = _dot(p.astype(BF16), wp_ref[g]) * sc_ref[:, sl]
        o_ref[:, sl] = y.astype(BF16)


def _pool_band_matrices():
    t = np.arange(TM_SEQ)[:, None]
    c = np.arange(TM_SEQ + 2 * POOL_HALO)[None, :] - POOL_HALO
    mats = [((c - t >= -(w // 2)) & (c - t <= w // 2 - 1)) for w in POOL_WINDOWS]
    return jnp.asarray(np.stack(mats).astype(np.float32), dtype=BF16)


def _pool_mixer(u, w_pool, pool_scale):
    per_halo = TM_SEQ // POOL_HALO
    n_halo = T_ALL // POOL_HALO
    return pl.pallas_call(
        _pool_kernel,
        grid=(T_ALL // TM_SEQ,),
        in_specs=[
            pl.BlockSpec((TM_SEQ, D_POOL), lambda i: (i, 0)),
            pl.BlockSpec((POOL_HALO, D_POOL), lambda i: (jnp.maximum(i * per_halo - 1, 0), 0)),
            pl.BlockSpec((POOL_HALO, D_POOL), lambda i: (jnp.minimum((i + 1) * per_halo, n_halo - 1), 0)),
            _const_spec((len(POOL_WINDOWS), TM_SEQ, TM_SEQ + 2 * POOL_HALO)),
            _const_spec((len(POOL_WINDOWS), POOL_GROUP, POOL_GROUP)),
            _const_spec((1, D_POOL)),
        ],
        out_specs=pl.BlockSpec((TM_SEQ, D_POOL), lambda i: (i, 0)),
        out_shape=jax.ShapeDtypeStruct((T_ALL, D_POOL), BF16),
        compiler_params=_cparams(), name="pool_mixer")(
            u, u, u, _pool_band_matrices(), w_pool, pool_scale)


ROWS_PER_STEP = TM_SEQ // GRID_W
KV_WINDOW_ROWS = 2 * ROWS_PER_STEP


def _na_step_geometry(s):
    steps_per_prompt = SEQ // TM_SEQ
    n_prompt_steps = T_PROMPT // TM_SEQ
    is_prompt = s < n_prompt_steps
    jj = jnp.where(is_prompt, lax.rem(s, steps_per_prompt), s - n_prompt_steps)
    rows = jnp.where(is_prompt, SEQ // GRID_W, DEC_SEQ // GRID_W)
    seq_row0 = jnp.where(is_prompt, (s // steps_per_prompt) * (SEQ // GRID_W), T_PROMPT // GRID_W)
    return jj * ROWS_PER_STEP, rows, seq_row0


def _na_window_row(s):
    r0, rows, seq_row0 = _na_step_geometry(s)
    return jnp.clip(r0 - NA_ROWS // 2, 0, rows - KV_WINDOW_ROWS), seq_row0


def _na_kernel(q_ref, k_ref, v_ref, bias_ref, o_ref):
    s = pl.program_id(0)
    r0, rows, _ = _na_step_geometry(s)
    win0, _ = _na_window_row(s)
    gl = HEADS_PER_GROUP * NA_HEAD_DIM
    lane = lax.broadcasted_iota(jnp.int32, (GRID_W, gl), 1)
    n_keys = NA_ROWS * GRID_W

    def row_body(i, carry):
        r = r0 + i
        row_start = jnp.clip(r - NA_ROWS // 2, 0, rows - NA_ROWS)
        k0 = pl.multiple_of((row_start - win0) * GRID_W, GRID_W)
        d0 = NA_ROWS - 1 + row_start - r
        q0 = pl.multiple_of(i * GRID_W, GRID_W)
        for p in range(NA_HEADS // HEADS_PER_GROUP):
            ls = slice(p * gl, (p + 1) * gl)
            q2 = q_ref[pl.ds(q0, GRID_W), ls] * jnp.asarray(NA_HEAD_DIM ** -0.5, BF16)
            zq = jnp.zeros_like(q2)
            qs = jnp.concatenate(
                [jnp.where((lane >= h * NA_HEAD_DIM) & (lane < (h + 1) * NA_HEAD_DIM), q2, zq)
                 for h in range(HEADS_PER_GROUP)], axis=0)
            kw = k_ref[pl.ds(k0, n_keys), ls]
            vw = v_ref[pl.ds(k0, n_keys), ls]
            sc = _dot_nt(qs, kw)
            sc = jnp.concatenate(
                [sc[:, j * LANES:(j + 1) * LANES] + bias_ref[p, d0 + 2 * j] for j in range(n_keys // LANES)],
                axis=1)
            m = jnp.max(sc, axis=-1, keepdims=True)
            e = jnp.exp(sc - m)
            den = jnp.sum(e, axis=-1, keepdims=True)
            o = _dot(e.astype(BF16), vw) / den
            out = o[0:GRID_W]
            for h in range(1, HEADS_PER_GROUP):
                out = jnp.where(lane >= h * NA_HEAD_DIM, o[h * GRID_W:(h + 1) * GRID_W], out)
            o_ref[pl.ds(q0, GRID_W), ls] = out.astype(BF16)
        return carry

    lax.fori_loop(0, ROWS_PER_STEP, row_body, 0, unroll=4)


def _na_bias_table(rpb):
    qc = np.arange(GRID_W)[:, None]
    kc = np.arange(GRID_W)[None, :]
    dc = np.clip(kc - qc + NA_COLS - 1, 0, 2 * NA_COLS - 2)
    col_start = np.clip(qc - NA_COLS // 2, 0, GRID_W - NA_COLS)
    mask = (kc - col_start >= 0) & (kc - col_start < NA_COLS)
    tab = jnp.where(mask[None, None], rpb[:, :, dc].astype(F32), NEG_INF)
    n_dr = 2 * NA_ROWS - 1
    ng = NA_HEADS // HEADS_PER_GROUP
    tab = tab.reshape(ng, HEADS_PER_GROUP, n_dr, GRID_W, GRID_W).transpose(0, 2, 1, 3, 4)
    tab = tab.reshape(ng, n_dr, HEADS_PER_GROUP * GRID_W, GRID_W)
    return jnp.concatenate([tab[:, :-1], tab[:, 1:]], axis=-1)


def _neighbourhood_attention(u, rpb):
    def kv_spec(col_block):
        def index(s):
            win0, seq_row0 = _na_window_row(s)
            return ((seq_row0 + win0) * GRID_W, col_block * D_NA)
        return pl.BlockSpec((pl.Element(KV_WINDOW_ROWS * GRID_W), pl.Element(D_NA)), index)

    bias = _na_bias_table(rpb)
    return pl.pallas_call(
        _na_kernel,
        grid=(T_ALL // TM_SEQ,),
        in_specs=[pl.BlockSpec((TM_SEQ, D_NA), lambda s: (s, 1)),
                  kv_spec(2), kv_spec(3), _const_spec(bias.shape)],
        out_specs=pl.BlockSpec((TM_SEQ, D_NA), lambda s: (s, 0)),
        out_shape=jax.ShapeDtypeStruct((T_ALL, D_NA), BF16),
        compiler_params=_cparams(), name="neighbourhood_attention")(u, u, u, bias)


def _mix_ln_kernel(ya_ref, yb_ref, xp_ref, xs_ref, w_ref, g_ref, b_ref, o_ref, *, n_prompt_tiles):
    i = pl.program_id(0)
    for rs in _sub_tiles(TM_ROW):
        x = jnp.where(i < n_prompt_tiles, xp_ref[rs, :], xs_ref[rs, :])
        mix = _dot(ya_ref[rs, :], w_ref[:D_POOL, :]) + _dot(yb_ref[rs, :], w_ref[D_POOL:, :])
        o_ref[rs, :] = _layer_norm(DEEPNORM_ALPHA * x + mix, g_ref[...], b_ref[...])


def _mix_ln(ya, yb, xp, xs, w_out, g, b):
    npt = T_PROMPT // TM_ROW
    spec_p, spec_s = _two_group_specs(TM_ROW, npt)
    return pl.pallas_call(
        functools.partial(_mix_ln_kernel, n_prompt_tiles=npt),
        grid=(T_ALL // TM_ROW,),
        in_specs=[pl.BlockSpec((TM_ROW, D_POOL), lambda i: (i, 0)),
                  pl.BlockSpec((TM_ROW, D_NA), lambda i: (i, 0)),
                  spec_p, spec_s,
                  _const_spec((D_MODEL, D_MODEL)), _const_spec((1, D_MODEL)), _const_spec((1, D_MODEL))],
        out_specs=pl.BlockSpec((TM_ROW, D_MODEL), lambda i: (i, 0)),
        out_shape=jax.ShapeDtypeStruct((T_ALL, D_MODEL), F32),
        compiler_params=_cparams(), name="mix_ln1")(ya, yb, xp, xs, w_out, g, b)


def _kv_kernel(m_ref, w_ref, o_ref):
    o_ref[...] = _dot(m_ref[...], w_ref[...]).astype(BF16)


def _memory_kv(mem, w_xkv):
    n_mem_rows = mem.shape[0]
    tn = 1024
    return pl.pallas_call(
        _kv_kernel,
        grid=(2 * D_MODEL // tn,),
        in_specs=[_const_spec((n_mem_rows, D_MODEL)), pl.BlockSpec((D_MODEL, tn), lambda j: (0, j))],
        out_specs=pl.BlockSpec((n_mem_rows, tn), lambda j: (0, j)),
        out_shape=jax.ShapeDtypeStruct((n_mem_rows, 2 * D_MODEL), BF16),
        compiler_params=_cparams(), name="memory_kv")(mem, w_xkv)


def _xattn_kernel(x_ref, wq_ref, k_ref, v_ref, o_ref):
    for rs in _sub_tiles(TM_ROW):
        q = _dot(x_ref[rs, :].astype(BF16), wq_ref[...]).astype(BF16)
        for h in range(XA_HEADS):
            sl = slice(h * XA_HEAD_DIM, (h + 1) * XA_HEAD_DIM)
            sc = _dot_nt(q[:, sl], k_ref[:, sl]) * (XA_HEAD_DIM ** -0.5)
            m = jnp.max(sc, axis=-1, keepdims=True)
            e = jnp.exp(sc - m)
            den = jnp.sum(e, axis=-1, keepdims=True)
            o_ref[rs, sl] = (_dot(e.astype(BF16), v_ref[:, sl]) / den).astype(BF16)


def _mem_batch(i):
    return jnp.minimum((i * TM_ROW) // SEQ, BATCH)


def _cross_attention(x1, w_xq, kv):
    return pl.pallas_call(
        _xattn_kernel,
        grid=(T_ALL // TM_ROW,),
        in_specs=[pl.BlockSpec((TM_ROW, D_MODEL), lambda i: (i, 0)),
                  _const_spec((D_MODEL, D_MODEL)),
                  pl.BlockSpec((N_MEM, D_MODEL), lambda i: (_mem_batch(i), 0)),
                  pl.BlockSpec((N_MEM, D_MODEL), lambda i: (_mem_batch(i), 1))],
        out_specs=pl.BlockSpec((TM_ROW, D_MODEL), lambda i: (i, 0)),
        out_shape=jax.ShapeDtypeStruct((T_ALL, D_MODEL), BF16),
        compiler_params=_cparams(), name="cross_attention")(x1, w_xq, kv, kv)


def _route_kernel(a_ref, x1_ref, wo_ref, g_ref, b_ref, wr_ref, br_ref,
                  x2_ref, x2p_ref, idx_ref, gate_ref, rank_ref, cnt_ref, carry_ref):
    @pl.when(pl.program_id(0) == 0)
    def _():
        carry_ref[...] = jnp.zeros_like(carry_ref)

    projs = [_dot(a_ref[rs, :], wo_ref[...]) for rs in _sub_tiles(TM_ROW)]
    for rs, proj in zip(_sub_tiles(TM_ROW), projs):
        _route_sub_tile(rs, proj, x1_ref, g_ref, b_ref, wr_ref, br_ref,
                        x2_ref, x2p_ref, idx_ref, gate_ref, rank_ref, carry_ref)
    cnt_ref[...] = carry_ref[...]


def _route_sub_tile(rs, proj, x1_ref, g_ref, b_ref, wr_ref, br_ref,
                    x2_ref, x2p_ref, idx_ref, gate_ref, rank_ref, carry_ref):
    n = SUB_ROWS
    x2 = _layer_norm(DEEPNORM_ALPHA * x1_ref[rs, :] + proj, g_ref[...], b_ref[...])
    x2_ref[rs, :] = x2
    half = D_MODEL // 2
    packed = pltpu.pack_elementwise([x2[:, :half], x2[:, half:]], packed_dtype=BF16)
    for s in range(ROW_TILE):
        x2p_ref[pl.ds(rs.start * ROW_TILE + s, n, stride=ROW_TILE), :] = packed[:, s * LANES:(s + 1) * LANES]

    lane = lax.broadcasted_iota(jnp.int32, (n, LANES), 1)
    lanef = lane.astype(F32)
    logits = jnp.where(lane < N_EXPERTS, _dot(x2.astype(BF16), wr_ref[...]) + br_ref[...], -jnp.inf)
    top_v, top_i, hots = [], [], []
    for _ in range(TOP_K):
        m = jnp.max(logits, axis=-1, keepdims=True)
        first = jnp.min(jnp.where(logits == m, lanef, float(LANES)), axis=-1, keepdims=True)
        hot = lanef == first
        top_v.append(m)
        top_i.append(first)
        hots.append(hot)
        logits = jnp.where(hot, -jnp.inf, logits)
    ex = [jnp.exp(v - top_v[0]) for v in top_v]
    den = ex[0] + ex[1] + ex[2] + ex[3]

    chosen = jnp.zeros((n, LANES), F32)
    for hot in hots:
        chosen = jnp.where(hot, 1.0, chosen)
    r_i = lax.broadcasted_iota(jnp.int32, (n, n), 0)
    c_i = lax.broadcasted_iota(jnp.int32, (n, n), 1)
    before = jnp.where(c_i < r_i, 1.0, 0.0).astype(BF16)
    base = carry_ref[...] + _dot(before, chosen.astype(BF16))

    idx_out = jnp.zeros((n, LANES), jnp.int32)
    gate_out = jnp.zeros((n, LANES), F32)
    rank_out = jnp.zeros((n, LANES), jnp.int32)
    for k in range(TOP_K):
        rank_k = jnp.sum(jnp.where(hots[k], base, 0.0), axis=-1, keepdims=True)
        idx_out = jnp.where(lane == k, top_i[k].astype(jnp.int32), idx_out)
        gate_out = jnp.where(lane == k, ex[k] / den, gate_out)
        rank_out = jnp.where(lane == k, rank_k.astype(jnp.int32), rank_out)
    idx_ref[rs, :] = idx_out
    gate_ref[rs, :] = gate_out
    rank_ref[rs, :] = rank_out
    carry_ref[...] = carry_ref[...] + jnp.sum(chosen, axis=0, keepdims=True)


def _project_norm_route(attn, x1, w_xo, g, b, w_router, b_router):
    row = lambda w: pl.BlockSpec((TM_ROW, w), lambda i: (i, 0))
    return pl.pallas_call(
        _route_kernel,
        grid=(T_ALL // TM_ROW,),
        in_specs=[row(D_MODEL), row(D_MODEL), _const_spec((D_MODEL, D_MODEL)),
                  _const_spec((1, D_MODEL)), _const_spec((1, D_MODEL)),
                  _const_spec((D_MODEL, LANES)), _const_spec((1, LANES))],
        out_specs=[row(D_MODEL), pl.BlockSpec((TM_ROW * ROW_TILE, LANES), lambda i: (i, 0)),
                   row(LANES), row(LANES), row(LANES),
                   pl.BlockSpec((1, LANES), lambda i: (0, 0))],
        out_shape=[jax.ShapeDtypeStruct((T_ALL, D_MODEL), F32),
                   jax.ShapeDtypeStruct((T_ALL * ROW_TILE, LANES), jnp.uint32),
                   jax.ShapeDtypeStruct((T_ALL, LANES), jnp.int32),
                   jax.ShapeDtypeStruct((T_ALL, LANES), F32),
                   jax.ShapeDtypeStruct((T_ALL, LANES), jnp.int32),
                   jax.ShapeDtypeStruct((1, LANES), F32)],
        scratch_shapes=[pltpu.VMEM((1, LANES), F32)],
        compiler_params=_cparams(), name="proj_ln2_route")(attn, x1, w_xo, g, b, w_router, b_router)


def _gather_kernel(tok_ref, x2p_ref, o_ref, sem):
    def issue(q, carry):
        for r in range(DMA_GROUP):
            j = q * DMA_GROUP + r
            src = pl.multiple_of(tok_ref[j] * ROW_TILE, ROW_TILE)
            dst = pl.multiple_of(j * ROW_TILE, ROW_TILE)
            pltpu.make_async_copy(x2p_ref.at[pl.ds(src, ROW_TILE)], o_ref.at[pl.ds(dst, ROW_TILE)],
                                  sem).start(priority=r % 2)
        return carry

    lax.fori_loop(0, TS_GATHER // DMA_GROUP, issue, 0)
    pltpu.make_async_copy(x2p_ref.at[pl.ds(0, TS_GATHER * ROW_TILE)], o_ref, sem).wait()


def _gather_rows(slot_tok, x2p):
    return pl.pallas_call(
        _gather_kernel,
        grid=(N_SLOTS // TS_GATHER,),
        in_specs=[pl.BlockSpec((TS_GATHER,), lambda i: (i,), memory_space=pltpu.SMEM),
                  pl.BlockSpec(memory_space=pl.ANY)],
        out_specs=pl.BlockSpec((TS_GATHER * ROW_TILE, LANES), lambda i: (i, 0)),
        out_shape=jax.ShapeDtypeStruct((N_SLOTS * ROW_TILE, LANES), jnp.uint32),
        scratch_shapes=[pltpu.SemaphoreType.DMA],
        compiler_params=_cparams(), name="gather_rows")(slot_tok, x2p)


def _unpack_rows(x_ref, rs):
    lo, hi = [], []
    for s in range(ROW_TILE):
        slab = x_ref[pl.ds(rs.start * ROW_TILE + s, rs.stop - rs.start, stride=ROW_TILE), :]
        lo.append(pltpu.unpack_elementwise(slab, index=0, packed_dtype=BF16, unpacked_dtype=F32).astype(BF16))
        hi.append(pltpu.unpack_elementwise(slab, index=1, packed_dtype=BF16, unpacked_dtype=F32).astype(BF16))
    return jnp.concatenate(lo, axis=1), jnp.concatenate(hi, axis=1)


STEP_VALID = 1
STEP_NEW_TILE = 2


def _gate_up_kernel(blk_ref, exp_ref, wcol_ref, ocol_ref, flag_ref, x_ref, wg_ref, wu_ref, bg_ref, bu_ref,
                    o_ref, wg_bf_ref, wu_bf_ref):
    flags = flag_ref[pl.program_id(0)]

    @pl.when((flags & STEP_NEW_TILE) != 0)
    def _():
        wg_bf_ref[...] = wg_ref[...].astype(BF16)
        wu_bf_ref[...] = wu_ref[...].astype(BF16)

    @pl.when((flags & STEP_VALID) != 0)
    def _():
        half = D_MODEL // 2
        x_lo, x_hi = _unpack_rows(x_ref, slice(0, TM_EXP))
        gate = _dot(x_lo, wg_bf_ref[:half, :]) + _dot(x_hi, wg_bf_ref[half:, :]) + bg_ref[0:1, :]
        up = _dot(x_lo, wu_bf_ref[:half, :]) + _dot(x_hi, wu_bf_ref[half:, :]) + bu_ref[0:1, :]
        gate = jnp.minimum(gate, SWIGLU_LIMIT)
        up = jnp.clip(up, -SWIGLU_LIMIT, SWIGLU_LIMIT)
        o_ref[...] = (gate * jax.nn.sigmoid(SWIGLU_ALPHA * gate) * (up + 1.0)).astype(BF16)

    @pl.when((flags & STEP_VALID) == 0)
    def _():
        o_ref[...] = jnp.zeros_like(o_ref)


def _down_kernel(blk_ref, exp_ref, wcol_ref, ocol_ref, flag_ref, a_ref, w_ref, b_ref, o_ref, w_bf_ref):
    flags = flag_ref[pl.program_id(0)]

    @pl.when((flags & STEP_NEW_TILE) != 0)
    def _():
        w_bf_ref[...] = w_ref[...].astype(BF16)

    @pl.when((flags & STEP_VALID) != 0)
    def _():
        o_ref[...] = _dot(a_ref[...], w_bf_ref[...]) + b_ref[0:1, :]

    @pl.when((flags & STEP_VALID) == 0)
    def _():
        o_ref[...] = jnp.zeros_like(o_ref)


def _pick(table, index):
    hot = index[..., None] == jnp.arange(table.shape[0], dtype=index.dtype)
    return jnp.sum(jnp.where(hot, table, 0), axis=-1)


def _expert_schedule(blocks_per_expert, n_col_tiles):
    n_steps = N_BLOCKS * n_col_tiles
    blk_end = jnp.cumsum(blocks_per_expert)
    blk_start = blk_end - blocks_per_expert
    total = blk_end[-1] * n_col_tiles
    step = jnp.arange(n_steps, dtype=jnp.int32)
    valid = step < total
    s = jnp.minimum(step, total - 1)
    e = jnp.sum((s[:, None] >= (blk_end * n_col_tiles)[None, :]).astype(jnp.int32), axis=1)
    start = _pick(blk_start, e)
    nb = _pick(blocks_per_expert, e)
    local = s - start * n_col_tiles
    col = local // nb
    blk = start + local - col * nb
    tile_id = e * n_col_tiles + col
    new_tile = valid & jnp.concatenate([jnp.ones((1,), bool), tile_id[1:] != tile_id[:-1]])
    flags = valid.astype(jnp.int32) * STEP_VALID + new_tile.astype(jnp.int32) * STEP_NEW_TILE
    spare = step - total
    blk = jnp.where(valid, blk, blk_end[-1] + spare // n_col_tiles)
    out_col = jnp.where(valid, col, spare % n_col_tiles)
    i32 = lambda a: a.astype(jnp.int32)
    return i32(blk), i32(e), i32(col), i32(out_col), flags


def _experts(xs, blocks_per_expert, w_gu, b_gu, w_down, b_down):
    rows = lambda s, blk, ex, wc, oc, fl: (blk[s], 0)
    out = lambda s, blk, ex, wc, oc, fl: (blk[s], oc[s])

    def weight(col_offset):
        return lambda s, blk, ex, wc, oc, fl: (ex[s], 0, col_offset + wc[s])

    n_gu = D_EXPERT // TN_GU
    sched = _expert_schedule(blocks_per_expert, n_gu)
    act = pl.pallas_call(
        _gate_up_kernel,
        grid_spec=pltpu.PrefetchScalarGridSpec(
            num_scalar_prefetch=5, grid=(N_BLOCKS * n_gu,),
            in_specs=[
                pl.BlockSpec((TM_EXP * ROW_TILE, LANES), rows),
                pl.BlockSpec((None, D_MODEL, TN_GU), weight(0)),
                pl.BlockSpec((None, D_MODEL, TN_GU), weight(n_gu)),
                pl.BlockSpec((None, SUBLANES, TN_GU), weight(0)),
                pl.BlockSpec((None, SUBLANES, TN_GU), weight(n_gu)),
            ],
            out_specs=pl.BlockSpec((TM_EXP, TN_GU), out),
            scratch_shapes=[pltpu.VMEM((D_MODEL, TN_GU), BF16), pltpu.VMEM((D_MODEL, TN_GU), BF16)]),
        out_shape=jax.ShapeDtypeStruct((N_SLOTS, D_EXPERT), BF16),
        compiler_params=_cparams(), name="expert_gate_up")(*sched, xs, w_gu, w_gu, b_gu, b_gu)

    n_dn = D_MODEL // TN_DOWN
    sched = _expert_schedule(blocks_per_expert, n_dn)
    return pl.pallas_call(
        _down_kernel,
        grid_spec=pltpu.PrefetchScalarGridSpec(
            num_scalar_prefetch=5, grid=(N_BLOCKS * n_dn,),
            in_specs=[
                pl.BlockSpec((TM_EXP, D_EXPERT), rows),
                pl.BlockSpec((None, D_EXPERT, TN_DOWN), weight(0)),
                pl.BlockSpec((None, SUBLANES, TN_DOWN), weight(0)),
            ],
            out_specs=pl.BlockSpec((TM_EXP, TN_DOWN), out),
            scratch_shapes=[pltpu.VMEM((D_EXPERT, TN_DOWN), BF16)]),
        out_shape=jax.ShapeDtypeStruct((N_SLOTS, D_MODEL), F32),
        compiler_params=_cparams(), name="expert_down")(*sched, act, w_down, b_down)


def _combine_kernel(dest_ref, gate_ref, x2_ref, y_ref, g_ref, b_ref, o_ref, buf_ref, sem):
    def issue(q, carry):
        for r in range(SUBLANES):
            t = q * SUBLANES + r
            for k in range(TOP_K):
                pltpu.make_async_copy(y_ref.at[pl.ds(dest_ref[t * TOP_K + k], 1)],
                                      buf_ref.at[k, q, pl.ds(r, 1)], sem).start(priority=k % 2)
        return carry

    n_tiles = TT_COMB // SUBLANES
    lax.fori_loop(0, n_tiles, issue, 0)
    for k in range(TOP_K):
        for q in range(n_tiles):
            pltpu.make_async_copy(y_ref.at[pl.ds(0, SUBLANES)], buf_ref.at[k, q], sem).wait()
    for rs in _sub_tiles(TT_COMB):
        gates = gate_ref[rs, :]
        tiles = slice(rs.start // SUBLANES, rs.stop // SUBLANES)
        plane = lambda k: buf_ref[k, tiles].reshape(SUB_ROWS, D_MODEL)
        h = gates[:, 0:1] * plane(0)
        for k in range(1, TOP_K):
            h = h + gates[:, k:k + 1] * plane(k)
        o_ref[rs, :] = _layer_norm(DEEPNORM_ALPHA * x2_ref[rs, :] + h, g_ref[...], b_ref[...])


def _combine(dest_flat, gates, x2, y_slots, g, b, first_tile, n_tiles):
    n = TT_COMB * TOP_K
    return pl.pallas_call(
        _combine_kernel,
        grid=(n_tiles,),
        in_specs=[pl.BlockSpec((n,), lambda i: (i + first_tile,), memory_space=pltpu.SMEM),
                  pl.BlockSpec((TT_COMB, LANES), lambda i: (i + first_tile, 0)),
                  pl.BlockSpec((TT_COMB, D_MODEL), lambda i: (i + first_tile, 0)),
                  pl.BlockSpec(memory_space=pl.ANY),
                  _const_spec((1, D_MODEL)), _const_spec((1, D_MODEL))],
        out_specs=pl.BlockSpec((TT_COMB, D_MODEL), lambda i: (i, 0)),
        out_shape=jax.ShapeDtypeStruct((n_tiles * TT_COMB, D_MODEL), F32),
        scratch_shapes=[pltpu.VMEM((TOP_K, TT_COMB // SUBLANES, SUBLANES, D_MODEL), F32),
                        pltpu.SemaphoreType.DMA],
        compiler_params=_cparams(), name="combine_ln3")(dest_flat, gates, x2, y_slots, g, b)


def _slot_tokens(top_i, counts, blocks_per_expert):
    n_assign = T_ALL * TOP_K
    flat = jnp.arange(n_assign, dtype=jnp.int32)
    _, order = lax.sort((top_i.reshape(n_assign), flat), num_keys=1, is_stable=True)
    sorted_tok = order // TOP_K
    blk_end = jnp.cumsum(blocks_per_expert)
    group_start = jnp.cumsum(counts) - counts
    blk = jnp.arange(N_BLOCKS, dtype=jnp.int32)
    e_blk = jnp.minimum(jnp.sum((blk[:, None] >= blk_end[None, :]).astype(jnp.int32), axis=1), N_EXPERTS - 1)
    first_blk = _pick(blk_end - blocks_per_expert, e_blk)
    per_slot = lambda v: jnp.repeat(v, TM_EXP)
    pos = per_slot((blk - first_blk) * TM_EXP) + jnp.tile(jnp.arange(TM_EXP, dtype=jnp.int32), N_BLOCKS)
    valid = per_slot(blk < blk_end[-1]) & (pos < per_slot(_pick(counts, e_blk)))
    src = jnp.clip(per_slot(_pick(group_start, e_blk)) + pos, 0, n_assign - 1)
    spread = jnp.arange(N_SLOTS, dtype=jnp.int32) % T_ALL
    return jnp.where(valid, sorted_tok[src], spread)


def kernel(x_prompt, x_sample, mem_prompt, mem_sample, w_in, w_pool, pool_scale, rpb, w_out,
           ln1_g, ln1_b, w_xq, w_xkv, w_xo, ln2_g, ln2_b,
           w_router, b_router, w_gu, b_gu, w_down, b_down, ln3_g, ln3_b):
    assert w_in.shape[0] == 1, "single-layer problem"
    xp = x_prompt.reshape(T_PROMPT, D_MODEL)
    xs = x_sample.reshape(T_SAMPLE, D_MODEL)
    mem = jnp.concatenate([mem_prompt.reshape(BATCH * N_MEM, D_MODEL),
                           mem_sample.reshape(N_MEM, D_MODEL)], axis=0).astype(BF16)
    row = lambda v: v.reshape(1, -1).astype(F32)

    u = _inproj(xp, xs, w_in[0].astype(BF16))
    ya = _pool_mixer(u, w_pool[0].astype(BF16), row(pool_scale[0]))
    yb = _neighbourhood_attention(u, rpb[0])
    x1 = _mix_ln(ya, yb, xp, xs, w_out[0].astype(BF16), row(ln1_g[0]), row(ln1_b[0]))

    kv = _memory_kv(mem, w_xkv[0].astype(BF16))
    attn = _cross_attention(x1, w_xq[0].astype(BF16), kv)
    w_r = jnp.pad(w_router[0], ((0, 0), (0, LANES - N_EXPERTS))).astype(BF16)
    b_r = jnp.pad(b_router[0].astype(F32), (0, LANES - N_EXPERTS)).reshape(1, LANES)
    x2, x2p, top_i, gates, rank, counts = _project_norm_route(
        attn, x1, w_xo[0].astype(BF16), row(ln2_g[0]), row(ln2_b[0]), w_r, b_r)

    counts = counts[0, :N_EXPERTS].astype(jnp.int32)
    blocks_per_expert = (counts + TM_EXP - 1) // TM_EXP
    blk_end = jnp.cumsum(blocks_per_expert)
    slot_start = (blk_end - blocks_per_expert) * TM_EXP
    dest = (_pick(slot_start, top_i[:, :TOP_K]) + rank[:, :TOP_K]).reshape(T_ALL * TOP_K)
    slot_tok = _slot_tokens(top_i[:, :TOP_K], counts, blocks_per_expert)

    xs_slots = _gather_rows(slot_tok, x2p)
    tiled = lambda bias: jnp.broadcast_to(bias[:, None, :], (N_EXPERTS, SUBLANES, bias.shape[-1]))
    y_slots = _experts(xs_slots, blocks_per_expert, w_gu[0], tiled(b_gu[0]), w_down[0], tiled(b_down[0]))

    g3, b3 = row(ln3_g[0]), row(ln3_b[0])
    n_p = T_PROMPT // TT_COMB
    y_prompt = _combine(dest, gates, x2, y_slots, g3, b3, 0, n_p)
    y_sample = _combine(dest, gates, x2, y_slots, g3, b3, n_p, T_SAMPLE // TT_COMB)
    return (y_prompt.reshape(BATCH, SEQ, D_MODEL), y_sample.reshape(1, DEC_SEQ, D_MODEL))
```

```python
import functools

import numpy as np
import jax
import jax.numpy as jnp
from jax import lax
from jax.experimental import pallas as pl
from jax.experimental.pallas import tpu as pltpu

F32 = jnp.float32
BF16 = jnp.bfloat16

D_MODEL = 2048
BATCH, SEQ = 4, 4096
DEC_SEQ = 8192
T_PROMPT = BATCH * SEQ
T_SAMPLE = DEC_SEQ
T_ALL = T_PROMPT + T_SAMPLE
GRID_W = 64
D_POOL = 1024
POOL_WINDOWS = (2, 4, 8, 16)
POOL_GROUP = 256
D_NA = 1024
NA_HEADS = 16
NA_HEAD_DIM = 64
NA_ROWS = 8
NA_COLS = 16
D_IN = D_POOL + 3 * D_NA
N_MEM = 256
XA_HEADS = 4
XA_HEAD_DIM = 512
N_EXPERTS = 32
TOP_K = 4
D_EXPERT = 2048
SWIGLU_LIMIT = 7.0
SWIGLU_ALPHA = 1.702
LN_EPS = 1e-5
DEEPNORM_ALPHA = 2.0 ** 0.25
NEG_INF = -1e30

LANES = 128
SUBLANES = 8
ROW_TILE = D_MODEL // 2 // LANES
DMA_GROUP = 8
VMEM_LIMIT = 56 * 1024 * 1024
TM_IN = 512
TM_SEQ = 512
POOL_HALO = 16
TM_ROW = 512
SUB_ROWS = 256
TM_EXP = 512
TN_GU = 1024
TN_DOWN = 1024
TS_GATHER = 4096
TT_COMB = 512
HEADS_PER_GROUP = 4
N_SLOTS = T_ALL * TOP_K + N_EXPERTS * TM_EXP
N_BLOCKS = N_SLOTS // TM_EXP


def _cparams(n_axes=1):
    return pltpu.CompilerParams(dimension_semantics=("arbitrary",) * n_axes,
                                vmem_limit_bytes=VMEM_LIMIT)


def _dot(a, b):
    return jnp.dot(a, b, preferred_element_type=F32)


def _dot_nt(a, b):
    return lax.dot_general(a, b, (((1,), (1,)), ((), ())), preferred_element_type=F32)


def _layer_norm(x, g, b):
    mu = jnp.mean(x, axis=-1, keepdims=True)
    xc = x - mu
    var = jnp.mean(xc * xc, axis=-1, keepdims=True)
    return xc * lax.rsqrt(var + LN_EPS) * g + b


def _sub_tiles(n_rows):
    return [slice(r, r + SUB_ROWS) for r in range(0, n_rows, SUB_ROWS)]


def _const_spec(shape):
    nd = len(shape)
    return pl.BlockSpec(shape, lambda *_: (0,) * nd, pipeline_mode=pl.Buffered(1))


def _inproj_kernel(xp_ref, xs_ref, w_ref, o_ref, *, n_prompt_tiles):
    i = pl.program_id(0)
    x = jnp.where(i < n_prompt_tiles, xp_ref[...], xs_ref[...]).astype(BF16)
    for c in range(D_IN // 1024):
        sl = slice(c * 1024, (c + 1) * 1024)
        o_ref[:, sl] = _dot(x, w_ref[:, sl]).astype(BF16)


def _two_group_specs(tm, n_prompt_tiles):
    last = n_prompt_tiles - 1
    return (pl.BlockSpec((tm, D_MODEL), lambda i: (jnp.minimum(i, last), 0)),
            pl.BlockSpec((tm, D_MODEL), lambda i: (jnp.maximum(i - n_prompt_tiles, 0), 0)))


def _inproj(xp, xs, w_in):
    npt = T_PROMPT // TM_IN
    spec_p, spec_s = _two_group_specs(TM_IN, npt)
    return pl.pallas_call(
        functools.partial(_inproj_kernel, n_prompt_tiles=npt),
        grid=(T_ALL // TM_IN,),
        in_specs=[spec_p, spec_s, _const_spec((D_MODEL, D_IN))],
        out_specs=pl.BlockSpec((TM_IN, D_IN), lambda i: (i, 0)),
        out_shape=jax.ShapeDtypeStruct((T_ALL, D_IN), BF16),
        compiler_params=_cparams(), name="inproj")(xp, xs, w_in)


def _seq_position(i):
    per_prompt = SEQ // TM_SEQ
    n_prompt = T_PROMPT // TM_SEQ
    is_prompt = i < n_prompt
    pos = jnp.where(is_prompt, lax.rem(i, per_prompt), i - n_prompt)
    nblk = jnp.where(is_prompt, per_prompt, DEC_SEQ // TM_SEQ)
    return pos, nblk


def _pool_kernel(u_ref, prev_ref, next_ref, a_ref, wp_ref, sc_ref, o_ref):
    pos, nblk = _seq_position(pl.program_id(0))
    cur = u_ref[...]
    zero = jnp.zeros((POOL_HALO, D_POOL), BF16)
    prev = jnp.where(pos == 0, zero, prev_ref[...])
    nxt = jnp.where(pos == nblk - 1, zero, next_ref[...])
    ext = jnp.concatenate([prev, cur, nxt], axis=0)
    t = pos * TM_SEQ + lax.broadcasted_iota(jnp.int32, (TM_SEQ, 1), 0)
    seq_len = nblk * TM_SEQ
    for g, w in enumerate(POOL_WINDOWS):
        sl = slice(g * POOL_GROUP, (g + 1) * POOL_GROUP)
        win_sum = _dot(a_ref[g], ext[:, sl])
        cnt = (jnp.minimum(t - w // 2 + w, seq_len) - jnp.maximum(t - w // 2, 0)).astype(F32)
        p = win_sum / cnt - cur[:, sl].astype(F32)
        y = _dot(p.astype(BF16), wp_ref[g]) * sc_ref[:, sl]
        o_ref[:, sl] = y.astype(BF16)


def _pool_band_matrices():
    t = np.arange(TM_SEQ)[:, None]
    c = np.arange(TM_SEQ + 2 * POOL_HALO)[None, :] - POOL_HALO
    mats = [((c - t >= -(w // 2)) & (c - t <= w // 2 - 1)) for w in POOL_WINDOWS]
    return jnp.asarray(np.stack(mats).astype(np.float32), dtype=BF16)


def _pool_mixer(u, w_pool, pool_scale):
    per_halo = TM_SEQ // POOL_HALO
    n_halo = T_ALL // POOL_HALO
    return pl.pallas_call(
        _pool_kernel,
        grid=(T_ALL // TM_SEQ,),
        in_specs=[
            pl.BlockSpec((TM_SEQ, D_POOL), lambda i: (i, 0)),
            pl.BlockSpec((POOL_HALO, D_POOL), lambda i: (jnp.maximum(i * per_halo - 1, 0), 0)),
            pl.BlockSpec((POOL_HALO, D_POOL), lambda i: (jnp.minimum((i + 1) * per_halo, n_halo - 1), 0)),
            _const_spec((len(POOL_WINDOWS), TM_SEQ, TM_SEQ + 2 * POOL_HALO)),
            _const_spec((len(POOL_WINDOWS), POOL_GROUP, POOL_GROUP)),
            _const_spec((1, D_POOL)),
        ],
        out_specs=pl.BlockSpec((TM_SEQ, D_POOL), lambda i: (i, 0)),
        out_shape=jax.ShapeDtypeStruct((T_ALL, D_POOL), BF16),
        compiler_params=_cparams(), name="pool_mixer")(
            u, u, u, _pool_band_matrices(), w_pool, pool_scale)


ROWS_PER_STEP = TM_SEQ // GRID_W
KV_WINDOW_ROWS = 2 * ROWS_PER_STEP


def _na_step_geometry(s):
    steps_per_prompt = SEQ // TM_SEQ
    n_prompt_steps = T_PROMPT // TM_SEQ
    is_prompt = s < n_prompt_steps
    jj = jnp.where(is_prompt, lax.rem(s, steps_per_prompt), s - n_prompt_steps)
    rows = jnp.where(is_prompt, SEQ // GRID_W, DEC_SEQ // GRID_W)
    seq_row0 = jnp.where(is_prompt, (s // steps_per_prompt) * (SEQ // GRID_W), T_PROMPT // GRID_W)
    return jj * ROWS_PER_STEP, rows, seq_row0


def _na_window_row(s):
    r0, rows, seq_row0 = _na_step_geometry(s)
    return jnp.clip(r0 - NA_ROWS // 2, 0, rows - KV_WINDOW_ROWS), seq_row0


def _na_kernel(q_ref, k_ref, v_ref, bias_ref, o_ref):
    s = pl.program_id(0)
    r0, rows, _ = _na_step_geometry(s)
    win0, _ = _na_window_row(s)
    gl = HEADS_PER_GROUP * NA_HEAD_DIM
    lane = lax.broadcasted_iota(jnp.int32, (GRID_W, gl), 1)
    n_keys = NA_ROWS * GRID_W

    def row_body(i, carry):
        r = r0 + i
        row_start = jnp.clip(r - NA_ROWS // 2, 0, rows - NA_ROWS)
        k0 = pl.multiple_of((row_start - win0) * GRID_W, GRID_W)
        d0 = NA_ROWS - 1 + row_start - r
        q0 = pl.multiple_of(i * GRID_W, GRID_W)
        for p in range(NA_HEADS // HEADS_PER_GROUP):
            ls = slice(p * gl, (p + 1) * gl)
            q2 = q_ref[pl.ds(q0, GRID_W), ls] * jnp.asarray(NA_HEAD_DIM ** -0.5, BF16)
            zq = jnp.zeros_like(q2)
            qs = jnp.concatenate(
                [jnp.where((lane >= h * NA_HEAD_DIM) & (lane < (h + 1) * NA_HEAD_DIM), q2, zq)
                 for h in range(HEADS_PER_GROUP)], axis=0)
            kw = k_ref[pl.ds(k0, n_keys), ls]
            vw = v_ref[pl.ds(k0, n_keys), ls]
            sc = _dot_nt(qs, kw)
            sc = jnp.concatenate(
                [sc[:, j * LANES:(j + 1) * LANES] + bias_ref[p, d0 + 2 * j] for j in range(n_keys // LANES)],
                axis=1)
            m = jnp.max(sc, axis=-1, keepdims=True)
            e = jnp.exp(sc - m)
            den = jnp.sum(e, axis=-1, keepdims=True)
            o = _dot(e.astype(BF16), vw) / den
            out = o[0:GRID_W]
            for h in range(1, HEADS_PER_GROUP):
                out = jnp.where(lane >= h * NA_HEAD_DIM, o[h * GRID_W:(h + 1) * GRID_W], out)
            o_ref[pl.ds(q0, GRID_W), ls] = out.astype(BF16)
        return carry

    lax.fori_loop(0, ROWS_PER_STEP, row_body, 0, unroll=8)


def _na_bias_table(rpb):
    qc = np.arange(GRID_W)[:, None]
    kc = np.arange(GRID_W)[None, :]
    dc = np.clip(kc - qc + NA_COLS - 1, 0, 2 * NA_COLS - 2)
    col_start = np.clip(qc - NA_COLS // 2, 0, GRID_W - NA_COLS)
    mask = (kc - col_start >= 0) & (kc - col_start < NA_COLS)
    tab = jnp.where(mask[None, None], rpb[:, :, dc].astype(F32), NEG_INF)
    n_dr = 2 * NA_ROWS - 1
    ng = NA_HEADS // HEADS_PER_GROUP
    tab = tab.reshape(ng, HEADS_PER_GROUP, n_dr, GRID_W, GRID_W).transpose(0, 2, 1, 3, 4)
    tab = tab.reshape(ng, n_dr, HEADS_PER_GROUP * GRID_W, GRID_W)
    return jnp.concatenate([tab[:, :-1], tab[:, 1:]], axis=-1)


def _neighbourhood_attention(u, rpb):
    def kv_spec(col_block):
        def index(s):
            win0, seq_row0 = _na_window_row(s)
            return ((seq_row0 + win0) * GRID_W, col_block * D_NA)
        return pl.BlockSpec((pl.Element(KV_WINDOW_ROWS * GRID_W), pl.Element(D_NA)), index)

    bias = _na_bias_table(rpb)
    return pl.pallas_call(
        _na_kernel,
        grid=(T_ALL // TM_SEQ,),
        in_specs=[pl.BlockSpec((TM_SEQ, D_NA), lambda s: (s, 1)),
                  kv_spec(2), kv_spec(3), _const_spec(bias.shape)],
        out_specs=pl.BlockSpec((TM_SEQ, D_NA), lambda s: (s, 0)),
        out_shape=jax.ShapeDtypeStruct((T_ALL, D_NA), BF16),
        compiler_params=_cparams(), name="neighbourhood_attention")(u, u, u, bias)


def _mix_ln_kernel(ya_ref, yb_ref, xp_ref, xs_ref, w_ref, g_ref, b_ref, o_ref, *, n_prompt_tiles):
    i = pl.program_id(0)
    for rs in _sub_tiles(TM_ROW):
        x = jnp.where(i < n_prompt_tiles, xp_ref[rs, :], xs_ref[rs, :])
        mix = _dot(ya_ref[rs, :], w_ref[:D_POOL, :]) + _dot(yb_ref[rs, :], w_ref[D_POOL:, :])
        o_ref[rs, :] = _layer_norm(DEEPNORM_ALPHA * x + mix, g_ref[...], b_ref[...])


def _mix_ln(ya, yb, xp, xs, w_out, g, b):
    npt = T_PROMPT // TM_ROW
    spec_p, spec_s = _two_group_specs(TM_ROW, npt)
    return pl.pallas_call(
        functools.partial(_mix_ln_kernel, n_prompt_tiles=npt),
        grid=(T_ALL // TM_ROW,),
        in_specs=[pl.BlockSpec((TM_ROW, D_POOL), lambda i: (i, 0)),
                  pl.BlockSpec((TM_ROW, D_NA), lambda i: (i, 0)),
                  spec_p, spec_s,
                  _const_spec((D_MODEL, D_MODEL)), _const_spec((1, D_MODEL)), _const_spec((1, D_MODEL))],
        out_specs=pl.BlockSpec((TM_ROW, D_MODEL), lambda i: (i, 0)),
        out_shape=jax.ShapeDtypeStruct((T_ALL, D_MODEL), F32),
        compiler_params=_cparams(), name="mix_ln1")(ya, yb, xp, xs, w_out, g, b)


def _kv_kernel(m_ref, w_ref, o_ref):
    o_ref[...] = _dot(m_ref[...], w_ref[...]).astype(BF16)


def _memory_kv(mem, w_xkv):
    n_mem_rows = mem.shape[0]
    tn = 1024
    return pl.pallas_call(
        _kv_kernel,
        grid=(2 * D_MODEL // tn,),
        in_specs=[_const_spec((n_mem_rows, D_MODEL)), pl.BlockSpec((D_MODEL, tn), lambda j: (0, j))],
        out_specs=pl.BlockSpec((n_mem_rows, tn), lambda j: (0, j)),
        out_shape=jax.ShapeDtypeStruct((n_mem_rows, 2 * D_MODEL), BF16),
        compiler_params=_cparams(), name="memory_kv")(mem, w_xkv)


def _xattn_kernel(x_ref, wq_ref, k_ref, v_ref, o_ref):
    for rs in _sub_tiles(TM_ROW):
        q = _dot(x_ref[rs, :].astype(BF16), wq_ref[...]).astype(BF16)
        for h in range(XA_HEADS):
            sl = slice(h * XA_HEAD_DIM, (h + 1) * XA_HEAD_DIM)
            sc = _dot_nt(q[:, sl], k_ref[:, sl]) * (XA_HEAD_DIM ** -0.5)
            m = jnp.max(sc, axis=-1, keepdims=True)
            e = jnp.exp(sc - m)
            den = jnp.sum(e, axis=-1, keepdims=True)
            o_ref[rs, sl] = (_dot(e.astype(BF16), v_ref[:, sl]) / den).astype(BF16)


def _mem_batch(i):
    return jnp.minimum((i * TM_ROW) // SEQ, BATCH)


def _cross_attention(x1, w_xq, kv):
    return pl.pallas_call(
        _xattn_kernel,
        grid=(T_ALL // TM_ROW,),
        in_specs=[pl.BlockSpec((TM_ROW, D_MODEL), lambda i: (i, 0)),
                  _const_spec((D_MODEL, D_MODEL)),
                  pl.BlockSpec((N_MEM, D_MODEL), lambda i: (_mem_batch(i), 0)),
                  pl.BlockSpec((N_MEM, D_MODEL), lambda i: (_mem_batch(i), 1))],
        out_specs=pl.BlockSpec((TM_ROW, D_MODEL), lambda i: (i, 0)),
        out_shape=jax.ShapeDtypeStruct((T_ALL, D_MODEL), BF16),
        compiler_params=_cparams(), name="cross_attention")(x1, w_xq, kv, kv)


def _route_kernel(a_ref, x1_ref, wo_ref, g_ref, b_ref, wr_ref, br_ref,
                  x2_ref, x2p_ref, idx_ref, gate_ref, rank_ref, cnt_ref, carry_ref):
    @pl.when(pl.program_id(0) == 0)
    def _():
        carry_ref[...] = jnp.zeros_like(carry_ref)

    projs = [_dot(a_ref[rs, :], wo_ref[...]) for rs in _sub_tiles(TM_ROW)]
    for rs, proj in zip(_sub_tiles(TM_ROW), projs):
        _route_sub_tile(rs, proj, x1_ref, g_ref, b_ref, wr_ref, br_ref,
                        x2_ref, x2p_ref, idx_ref, gate_ref, rank_ref, carry_ref)
    cnt_ref[...] = carry_ref[...]


def _route_sub_tile(rs, proj, x1_ref, g_ref, b_ref, wr_ref, br_ref,
                    x2_ref, x2p_ref, idx_ref, gate_ref, rank_ref, carry_ref):
    n = SUB_ROWS
    x2 = _layer_norm(DEEPNORM_ALPHA * x1_ref[rs, :] + proj, g_ref[...], b_ref[...])
    x2_ref[rs, :] = x2
    half = D_MODEL // 2
    packed = pltpu.pack_elementwise([x2[:, :half], x2[:, half:]], packed_dtype=BF16)
    for s in range(ROW_TILE):
        x2p_ref[pl.ds(rs.start * ROW_TILE + s, n, stride=ROW_TILE), :] = packed[:, s * LANES:(s + 1) * LANES]

    lane = lax.broadcasted_iota(jnp.int32, (n, LANES), 1)
    lanef = lane.astype(F32)
    logits = jnp.where(lane < N_EXPERTS, _dot(x2.astype(BF16), wr_ref[...]) + br_ref[...], -jnp.inf)
    top_v, top_i, hots = [], [], []
    for _ in range(TOP_K):
        m = jnp.max(logits, axis=-1, keepdims=True)
        first = jnp.min(jnp.where(logits == m, lanef, float(LANES)), axis=-1, keepdims=True)
        hot = lanef == first
        top_v.append(m)
        top_i.append(first)
        hots.append(hot)
        logits = jnp.where(hot, -jnp.inf, logits)
    ex = [jnp.exp(v - top_v[0]) for v in top_v]
    den = ex[0] + ex[1] + ex[2] + ex[3]

    chosen = jnp.zeros((n, LANES), F32)
    for hot in hots:
        chosen = jnp.where(hot, 1.0, chosen)
    r_i = lax.broadcasted_iota(jnp.int32, (n, n), 0)
    c_i = lax.broadcasted_iota(jnp.int32, (n, n), 1)
    before = jnp.where(c_i < r_i, 1.0, 0.0).astype(BF16)
    base = carry_ref[...] + _dot(before, chosen.astype(BF16))

    idx_out = jnp.zeros((n, LANES), jnp.int32)
    gate_out = jnp.zeros((n, LANES), F32)
    rank_out = jnp.zeros((n, LANES), jnp.int32)
    for k in range(TOP_K):
        rank_k = jnp.sum(jnp.where(hots[k], base, 0.0), axis=-1, keepdims=True)
        idx_out = jnp.where(lane == k, top_i[k].astype(jnp.int32), idx_out)
        gate_out = jnp.where(lane == k, ex[k] / den, gate_out)
        rank_out = jnp.where(lane == k, rank_k.astype(jnp.int32), rank_out)
    idx_ref[rs, :] = idx_out
    gate_ref[rs, :] = gate_out
    rank_ref[rs, :] = rank_out
    carry_ref[...] = carry_ref[...] + jnp.sum(chosen, axis=0, keepdims=True)


def _project_norm_route(attn, x1, w_xo, g, b, w_router, b_router):
    row = lambda w: pl.BlockSpec((TM_ROW, w), lambda i: (i, 0))
    return pl.pallas_call(
        _route_kernel,
        grid=(T_ALL // TM_ROW,),
        in_specs=[row(D_MODEL), row(D_MODEL), _const_spec((D_MODEL, D_MODEL)),
                  _const_spec((1, D_MODEL)), _const_spec((1, D_MODEL)),
                  _const_spec((D_MODEL, LANES)), _const_spec((1, LANES))],
        out_specs=[row(D_MODEL), pl.BlockSpec((TM_ROW * ROW_TILE, LANES), lambda i: (i, 0)),
                   row(LANES), row(LANES), row(LANES),
                   pl.BlockSpec((1, LANES), lambda i: (0, 0))],
        out_shape=[jax.ShapeDtypeStruct((T_ALL, D_MODEL), F32),
                   jax.ShapeDtypeStruct((T_ALL * ROW_TILE, LANES), jnp.uint32),
                   jax.ShapeDtypeStruct((T_ALL, LANES), jnp.int32),
                   jax.ShapeDtypeStruct((T_ALL, LANES), F32),
                   jax.ShapeDtypeStruct((T_ALL, LANES), jnp.int32),
                   jax.ShapeDtypeStruct((1, LANES), F32)],
        scratch_shapes=[pltpu.VMEM((1, LANES), F32)],
        compiler_params=_cparams(), name="proj_ln2_route")(attn, x1, w_xo, g, b, w_router, b_router)


def _gather_kernel(tok_ref, x2p_ref, o_ref, sem):
    def issue(q, carry):
        for r in range(DMA_GROUP):
            j = q * DMA_GROUP + r
            src = pl.multiple_of(tok_ref[j] * ROW_TILE, ROW_TILE)
            dst = pl.multiple_of(j * ROW_TILE, ROW_TILE)
            pltpu.make_async_copy(x2p_ref.at[pl.ds(src, ROW_TILE)], o_ref.at[pl.ds(dst, ROW_TILE)],
                                  sem).start(priority=r % 2)
        return carry

    lax.fori_loop(0, TS_GATHER // DMA_GROUP, issue, 0)
    pltpu.make_async_copy(x2p_ref.at[pl.ds(0, TS_GATHER * ROW_TILE)], o_ref, sem).wait()


def _gather_rows(slot_tok, x2p):
    return pl.pallas_call(
        _gather_kernel,
        grid=(N_SLOTS // TS_GATHER,),
        in_specs=[pl.BlockSpec((TS_GATHER,), lambda i: (i,), memory_space=pltpu.SMEM),
                  pl.BlockSpec(memory_space=pl.ANY)],
        out_specs=pl.BlockSpec((TS_GATHER * ROW_TILE, LANES), lambda i: (i, 0)),
        out_shape=jax.ShapeDtypeStruct((N_SLOTS * ROW_TILE, LANES), jnp.uint32),
        scratch_shapes=[pltpu.SemaphoreType.DMA],
        compiler_params=_cparams(), name="gather_rows")(slot_tok, x2p)


def _unpack_rows(x_ref, rs):
    lo, hi = [], []
    for s in range(ROW_TILE):
        slab = x_ref[pl.ds(rs.start * ROW_TILE + s, rs.stop - rs.start, stride=ROW_TILE), :]
        lo.append(pltpu.unpack_elementwise(slab, index=0, packed_dtype=BF16, unpacked_dtype=F32).astype(BF16))
        hi.append(pltpu.unpack_elementwise(slab, index=1, packed_dtype=BF16, unpacked_dtype=F32).astype(BF16))
    return jnp.concatenate(lo, axis=1), jnp.concatenate(hi, axis=1)


STEP_VALID = 1
STEP_NEW_TILE = 2


def _gate_up_kernel(blk_ref, exp_ref, wcol_ref, ocol_ref, flag_ref, x_ref, wg_ref, wu_ref, bg_ref, bu_ref,
                    o_ref, wg_bf_ref, wu_bf_ref):
    flags = flag_ref[pl.program_id(0)]

    @pl.when((flags & STEP_NEW_TILE) != 0)
    def _():
        wg_bf_ref[...] = wg_ref[...].astype(BF16)
        wu_bf_ref[...] = wu_ref[...].astype(BF16)

    @pl.when((flags & STEP_VALID) != 0)
    def _():
        half = D_MODEL // 2
        x_lo, x_hi = _unpack_rows(x_ref, slice(0, TM_EXP))
        gate = _dot(x_lo, wg_bf_ref[:half, :]) + _dot(x_hi, wg_bf_ref[half:, :]) + bg_ref[0:1, :]
        up = _dot(x_lo, wu_bf_ref[:half, :]) + _dot(x_hi, wu_bf_ref[half:, :]) + bu_ref[0:1, :]
        gate = jnp.minimum(gate, SWIGLU_LIMIT)
        up = jnp.clip(up, -SWIGLU_LIMIT, SWIGLU_LIMIT)
        o_ref[...] = (gate * jax.nn.sigmoid(SWIGLU_ALPHA * gate) * (up + 1.0)).astype(BF16)

    @pl.when((flags & STEP_VALID) == 0)
    def _():
        o_ref[...] = jnp.zeros_like(o_ref)


def _down_kernel(blk_ref, exp_ref, wcol_ref, ocol_ref, flag_ref, a_ref, w_ref, b_ref, o_ref, w_bf_ref):
    flags = flag_ref[pl.program_id(0)]

    @pl.when((flags & STEP_NEW_TILE) != 0)
    def _():
        w_bf_ref[...] = w_ref[...].astype(BF16)

    @pl.when((flags & STEP_VALID) != 0)
    def _():
        o_ref[...] = _dot(a_ref[...], w_bf_ref[...]) + b_ref[0:1, :]

    @pl.when((flags & STEP_VALID) == 0)
    def _():
        o_ref[...] = jnp.zeros_like(o_ref)


def _pick(table, index):
    hot = index[..., None] == jnp.arange(table.shape[0], dtype=index.dtype)
    return jnp.sum(jnp.where(hot, table, 0), axis=-1)


def _expert_schedule(blocks_per_expert, n_col_tiles):
    n_steps = N_BLOCKS * n_col_tiles
    blk_end = jnp.cumsum(blocks_per_expert)
    blk_start = blk_end - blocks_per_expert
    total = blk_end[-1] * n_col_tiles
    step = jnp.arange(n_steps, dtype=jnp.int32)
    valid = step < total
    s = jnp.minimum(step, total - 1)
    e = jnp.sum((s[:, None] >= (blk_end * n_col_tiles)[None, :]).astype(jnp.int32), axis=1)
    start = _pick(blk_start, e)
    nb = _pick(blocks_per_expert, e)
    local = s - start * n_col_tiles
    col = local // nb
    blk = start + local - col * nb
    tile_id = e * n_col_tiles + col
    new_tile = valid & jnp.concatenate([jnp.ones((1,), bool), tile_id[1:] != tile_id[:-1]])
    flags = valid.astype(jnp.int32) * STEP_VALID + new_tile.astype(jnp.int32) * STEP_NEW_TILE
    spare = step - total
    blk = jnp.where(valid, blk, blk_end[-1] + spare // n_col_tiles)
    out_col = jnp.where(valid, col, spare % n_col_tiles)
    i32 = lambda a: a.astype(jnp.int32)
    return i32(blk), i32(e), i32(col), i32(out_col), flags


def _experts(xs, blocks_per_expert, w_gu, b_gu, w_down, b_down):
    rows = lambda s, blk, ex, wc, oc, fl: (blk[s], 0)
    out = lambda s, blk, ex, wc, oc, fl: (blk[s], oc[s])

    def weight(col_offset):
        return lambda s, blk, ex, wc, oc, fl: (ex[s], 0, col_offset + wc[s])

    n_gu = D_EXPERT // TN_GU
    sched = _expert_schedule(blocks_per_expert, n_gu)
    act = pl.pallas_call(
        _gate_up_kernel,
        grid_spec=pltpu.PrefetchScalarGridSpec(
            num_scalar_prefetch=5, grid=(N_BLOCKS * n_gu,),
            in_specs=[
                pl.BlockSpec((TM_EXP * ROW_TILE, LANES), rows),
                pl.BlockSpec((None, D_MODEL, TN_GU), weight(0)),
                pl.BlockSpec((None, D_MODEL, TN_GU), weight(n_gu)),
                pl.BlockSpec((None, SUBLANES, TN_GU), weight(0)),
                pl.BlockSpec((None, SUBLANES, TN_GU), weight(n_gu)),
            ],
            out_specs=pl.BlockSpec((TM_EXP, TN_GU), out),
            scratch_shapes=[pltpu.VMEM((D_MODEL, TN_GU), BF16), pltpu.VMEM((D_MODEL, TN_GU), BF16)]),
        out_shape=jax.ShapeDtypeStruct((N_SLOTS, D_EXPERT), BF16),
        compiler_params=_cparams(), name="expert_gate_up")(*sched, xs, w_gu, w_gu, b_gu, b_gu)

    n_dn = D_MODEL // TN_DOWN
    sched = _expert_schedule(blocks_per_expert, n_dn)
    return pl.pallas_call(
        _down_kernel,
        grid_spec=pltpu.PrefetchScalarGridSpec(
            num_scalar_prefetch=5, grid=(N_BLOCKS * n_dn,),
            in_specs=[
                pl.BlockSpec((TM_EXP, D_EXPERT), rows),
                pl.BlockSpec((None, D_EXPERT, TN_DOWN), weight(0)),
                pl.BlockSpec((None, SUBLANES, TN_DOWN), weight(0)),
            ],
            out_specs=pl.BlockSpec((TM_EXP, TN_DOWN), out),
            scratch_shapes=[pltpu.VMEM((D_EXPERT, TN_DOWN), BF16)]),
        out_shape=jax.ShapeDtypeStruct((N_SLOTS, D_MODEL), F32),
        compiler_params=_cparams(), name="expert_down")(*sched, act, w_down, b_down)


def _combine_kernel(dest_ref, gate_ref, x2_ref, y_ref, g_ref, b_ref, o_ref, buf_ref, sem):
    def issue(q, carry):
        for r in range(SUBLANES):
            t = q * SUBLANES + r
            for k in range(TOP_K):
                pltpu.make_async_copy(y_ref.at[pl.ds(dest_ref[t * TOP_K + k], 1)],
                                      buf_ref.at[k, q, pl.ds(r, 1)], sem).start(priority=k % 2)
        return carry

    n_tiles = TT_COMB // SUBLANES
    lax.fori_loop(0, n_tiles, issue, 0)
    for k in range(TOP_K):
        for q in range(n_tiles):
            pltpu.make_async_copy(y_ref.at[pl.ds(0, SUBLANES)], buf_ref.at[k, q], sem).wait()
    for rs in _sub_tiles(TT_COMB):
        gates = gate_ref[rs, :]
        tiles = slice(rs.start // SUBLANES, rs.stop // SUBLANES)
        plane = lambda k: buf_ref[k, tiles].reshape(SUB_ROWS, D_MODEL)
        h = gates[:, 0:1] * plane(0)
        for k in range(1, TOP_K):
            h = h + gates[:, k:k + 1] * plane(k)
        o_ref[rs, :] = _layer_norm(DEEPNORM_ALPHA * x2_ref[rs, :] + h, g_ref[...], b_ref[...])


def _combine(dest_flat, gates, x2, y_slots, g, b, first_tile, n_tiles):
    n = TT_COMB * TOP_K
    return pl.pallas_call(
        _combine_kernel,
        grid=(n_tiles,),
        in_specs=[pl.BlockSpec((n,), lambda i: (i + first_tile,), memory_space=pltpu.SMEM),
                  pl.BlockSpec((TT_COMB, LANES), lambda i: (i + first_tile, 0)),
                  pl.BlockSpec((TT_COMB, D_MODEL), lambda i: (i + first_tile, 0)),
                  pl.BlockSpec(memory_space=pl.ANY),
                  _const_spec((1, D_MODEL)), _const_spec((1, D_MODEL))],
        out_specs=pl.BlockSpec((TT_COMB, D_MODEL), lambda i: (i, 0)),
        out_shape=jax.ShapeDtypeStruct((n_tiles * TT_COMB, D_MODEL), F32),
        scratch_shapes=[pltpu.VMEM((TOP_K, TT_COMB // SUBLANES, SUBLANES, D_MODEL), F32),
                        pltpu.SemaphoreType.DMA],
        compiler_params=_cparams(), name="combine_ln3")(dest_flat, gates, x2, y_slots, g, b)


def _slot_tokens(top_i, counts, blocks_per_expert):
    n_assign = T_ALL * TOP_K
    flat = jnp.arange(n_assign, dtype=jnp.int32)
    _, order = lax.sort((top_i.reshape(n_assign), flat), num_keys=1, is_stable=True)
    sorted_tok = order // TOP_K
    blk_end = jnp.cumsum(blocks_per_expert)
    group_start = jnp.cumsum(counts) - counts
    blk = jnp.arange(N_BLOCKS, dtype=jnp.int32)
    e_blk = jnp.minimum(jnp.sum((blk[:, None] >= blk_end[None, :]).astype(jnp.int32), axis=1), N_EXPERTS - 1)
    first_blk = _pick(blk_end - blocks_per_expert, e_blk)
    per_slot = lambda v: jnp.repeat(v, TM_EXP)
    pos = per_slot((blk - first_blk) * TM_EXP) + jnp.tile(jnp.arange(TM_EXP, dtype=jnp.int32), N_BLOCKS)
    valid = per_slot(blk < blk_end[-1]) & (pos < per_slot(_pick(counts, e_blk)))
    src = jnp.clip(per_slot(_pick(group_start, e_blk)) + pos, 0, n_assign - 1)
    spread = jnp.arange(N_SLOTS, dtype=jnp.int32) % T_ALL
    return jnp.where(valid, sorted_tok[src], spread)


def kernel(x_prompt, x_sample, mem_prompt, mem_sample, w_in, w_pool, pool_scale, rpb, w_out,
           ln1_g, ln1_b, w_xq, w_xkv, w_xo, ln2_g, ln2_b,
           w_router, b_router, w_gu, b_gu, w_down, b_down, ln3_g, ln3_b):
    assert w_in.shape[0] == 1, "single-layer problem"
    xp = x_prompt.reshape(T_PROMPT, D_MODEL)
    xs = x_sample.reshape(T_SAMPLE, D_MODEL)
    mem = jnp.concatenate([mem_prompt.reshape(BATCH * N_MEM, D_MODEL),
                           mem_sample.reshape(N_MEM, D_MODEL)], axis=0).astype(BF16)
    row = lambda v: v.reshape(1, -1).astype(F32)

    u = _inproj(xp, xs, w_in[0].astype(BF16))
    ya = _pool_mixer(u, w_pool[0].astype(BF16), row(pool_scale[0]))
    yb = _neighbourhood_attention(u, rpb[0])
    x1 = _mix_ln(ya, yb, xp, xs, w_out[0].astype(BF16), row(ln1_g[0]), row(ln1_b[0]))

    kv = _memory_kv(mem, w_xkv[0].astype(BF16))
    attn = _cross_attention(x1, w_xq[0].astype(BF16), kv)
    w_r = jnp.pad(w_router[0], ((0, 0), (0, LANES - N_EXPERTS))).astype(BF16)
    b_r = jnp.pad(b_router[0].astype(F32), (0, LANES - N_EXPERTS)).reshape(1, LANES)
    x2, x2p, top_i, gates, rank, counts = _project_norm_route(
        attn, x1, w_xo[0].astype(BF16), row(ln2_g[0]), row(ln2_b[0]), w_r, b_r)

    counts = counts[0, :N_EXPERTS].astype(jnp.int32)
    blocks_per_expert = (counts + TM_EXP - 1) // TM_EXP
    blk_end = jnp.cumsum(blocks_per_expert)
    slot_start = (blk_end - blocks_per_expert) * TM_EXP
    dest = (_pick(slot_start, top_i[:, :TOP_K]) + rank[:, :TOP_K]).reshape(T_ALL * TOP_K)
    slot_tok = _slot_tokens(top_i[:, :TOP_K], counts, blocks_per_expert)

    xs_slots = _gather_rows(slot_tok, x2p)
    tiled = lambda bias: jnp.broadcast_to(bias[:, None, :], (N_EXPERTS, SUBLANES, bias.shape[-1]))
    y_slots = _experts(xs_slots, blocks_per_expert, w_gu[0], tiled(b_gu[0]), w_down[0], tiled(b_down[0]))

    g3, b3 = row(ln3_g[0]), row(ln3_b[0])
    n_p = T_PROMPT // TT_COMB
    y_prompt = _combine(dest, gates, x2, y_slots, g3, b3, 0, n_p)
    y_sample = _combine(dest, gates, x2, y_slots, g3, b3, n_p, T_SAMPLE // TT_COMB)
    return (y_prompt.reshape(BATCH, SEQ, D_MODEL), y_sample.reshape(1, DEC_SEQ, D_MODEL))
```

```python
import functools

import numpy as np
import jax
import jax.numpy as jnp
from jax import lax
from jax.experimental import pallas as pl
from jax.experimental.pallas import tpu as pltpu

F32 = jnp.float32
BF16 = jnp.bfloat16

D_MODEL = 2048
BATCH, SEQ = 4, 4096
DEC_SEQ = 8192
T_PROMPT = BATCH * SEQ
T_SAMPLE = DEC_SEQ
T_ALL = T_PROMPT + T_SAMPLE
GRID_W = 64
D_POOL = 1024
POOL_WINDOWS = (2, 4, 8, 16)
POOL_GROUP = 256
D_NA = 1024
NA_HEADS = 16
NA_HEAD_DIM = 64
NA_ROWS = 8
NA_COLS = 16
D_IN = D_POOL + 3 * D_NA
N_MEM = 256
XA_HEADS = 4
XA_HEAD_DIM = 512
N_EXPERTS = 32
TOP_K = 4
D_EXPERT = 2048
SWIGLU_LIMIT = 7.0
SWIGLU_ALPHA = 1.702
LN_EPS = 1e-5
DEEPNORM_ALPHA = 2.0 ** 0.25
NEG_INF = -1e30

LANES = 128
SUBLANES = 8
ROW_TILE = D_MODEL // 2 // LANES
DMA_GROUP = 8
ROUTE_COLS = 8
VMEM_LIMIT = 56 * 1024 * 1024
TM_IN = 512
TM_SEQ = 512
POOL_HALO = 16
TM_ROW = 512
SUB_ROWS = 256
TM_EXP = 512
TN_GU = 1024
TN_DOWN = 1024
TS_GATHER = 4096
TT_COMB = 512
HEADS_PER_GROUP = 4
N_SLOTS = T_ALL * TOP_K + N_EXPERTS * TM_EXP
N_BLOCKS = N_SLOTS // TM_EXP


def _cparams(n_axes=1):
    return pltpu.CompilerParams(dimension_semantics=("arbitrary",) * n_axes,
                                vmem_limit_bytes=VMEM_LIMIT)


def _dot(a, b):
    return jnp.dot(a, b, preferred_element_type=F32)


def _dot_nt(a, b):
    return lax.dot_general(a, b, (((1,), (1,)), ((), ())), preferred_element_type=F32)


def _layer_norm(x, g, b):
    mu = jnp.mean(x, axis=-1, keepdims=True)
    xc = x - mu
    var = jnp.mean(xc * xc, axis=-1, keepdims=True)
    return xc * lax.rsqrt(var + LN_EPS) * g + b


def _sub_tiles(n_rows):
    return [slice(r, r + SUB_ROWS) for r in range(0, n_rows, SUB_ROWS)]


def _const_spec(shape):
    nd = len(shape)
    return pl.BlockSpec(shape, lambda *_: (0,) * nd, pipeline_mode=pl.Buffered(1))


def _inproj_kernel(xp_ref, xs_ref, w_ref, o_ref, *, n_prompt_tiles):
    i = pl.program_id(0)
    x = jnp.where(i < n_prompt_tiles, xp_ref[...], xs_ref[...]).astype(BF16)
    for c in range(D_IN // 1024):
        sl = slice(c * 1024, (c + 1) * 1024)
        o_ref[:, sl] = _dot(x, w_ref[:, sl]).astype(BF16)


def _two_group_specs(tm, n_prompt_tiles):
    last = n_prompt_tiles - 1
    return (pl.BlockSpec((tm, D_MODEL), lambda i: (jnp.minimum(i, last), 0)),
            pl.BlockSpec((tm, D_MODEL), lambda i: (jnp.maximum(i - n_prompt_tiles, 0), 0)))


def _inproj(xp, xs, w_in):
    npt = T_PROMPT // TM_IN
    spec_p, spec_s = _two_group_specs(TM_IN, npt)
    return pl.pallas_call(
        functools.partial(_inproj_kernel, n_prompt_tiles=npt),
        grid=(T_ALL // TM_IN,),
        in_specs=[spec_p, spec_s, _const_spec((D_MODEL, D_IN))],
        out_specs=pl.BlockSpec((TM_IN, D_IN), lambda i: (i, 0)),
        out_shape=jax.ShapeDtypeStruct((T_ALL, D_IN), BF16),
        compiler_params=_cparams(), name="inproj")(xp, xs, w_in)


def _seq_position(i):
    per_prompt = SEQ // TM_SEQ
    n_prompt = T_PROMPT // TM_SEQ
    is_prompt = i < n_prompt
    pos = jnp.where(is_prompt, lax.rem(i, per_prompt), i - n_prompt)
    nblk = jnp.where(is_prompt, per_prompt, DEC_SEQ // TM_SEQ)
    return pos, nblk


def _pool_kernel(u_ref, prev_ref, next_ref, a_ref, wp_ref, sc_ref, o_ref):
    pos, nblk = _seq_position(pl.program_id(0))
    cur = u_ref[...]
    zero = jnp.zeros((POOL_HALO, D_POOL), BF16)
    prev = jnp.where(pos == 0, zero, prev_ref[...])
    nxt = jnp.where(pos == nblk - 1, zero, next_ref[...])
    ext = jnp.concatenate([prev, cur, nxt], axis=0)
    t = pos * TM_SEQ + lax.broadcasted_iota(jnp.int32, (TM_SEQ, 1), 0)
    seq_len = nblk * TM_SEQ
    for g, w in enumerate(POOL_WINDOWS):
        sl = slice(g * POOL_GROUP, (g + 1) * POOL_GROUP)
        win_sum = _dot(a_ref[g], ext[:, sl])
        cnt = (jnp.minimum(t - w // 2 + w, seq_len) - jnp.maximum(t - w // 2, 0)).astype(F32)
        p = win_sum / cnt - cur[:, sl].astype(F32)
        y = _dot(p.astype(BF16), wp_ref[g]) * sc_ref[:, sl]
        o_ref[:, sl] = y.astype(BF16)


def _pool_band_matrices():
    t = np.arange(TM_SEQ)[:, None]
    c = np.arange(TM_SEQ + 2 * POOL_HALO)[None, :] - POOL_HALO
    mats = [((c - t >= -(w // 2)) & (c - t <= w // 2 - 1)) for w in POOL_WINDOWS]
    return jnp.asarray(np.stack(mats).astype(np.float32), dtype=BF16)


def _pool_mixer(u, w_pool, pool_scale):
    per_halo = TM_SEQ // POOL_HALO
    n_halo = T_ALL // POOL_HALO
    return pl.pallas_call(
        _pool_kernel,
        grid=(T_ALL // TM_SEQ,),
        in_specs=[
            pl.BlockSpec((TM_SEQ, D_POOL), lambda i: (i, 0)),
            pl.BlockSpec((POOL_HALO, D_POOL), lambda i: (jnp.maximum(i * per_halo - 1, 0), 0)),
            pl.BlockSpec((POOL_HALO, D_POOL), lambda i: (jnp.minimum((i + 1) * per_halo, n_halo - 1), 0)),
            _const_spec((len(POOL_WINDOWS), TM_SEQ, TM_SEQ + 2 * POOL_HALO)),
            _const_spec((len(POOL_WINDOWS), POOL_GROUP, POOL_GROUP)),
            _const_spec((1, D_POOL)),
        ],
        out_specs=pl.BlockSpec((TM_SEQ, D_POOL), lambda i: (i, 0)),
        out_shape=jax.ShapeDtypeStruct((T_ALL, D_POOL), BF16),
        compiler_params=_cparams(), name="pool_mixer")(
            u, u, u, _pool_band_matrices(), w_pool, pool_scale)


ROWS_PER_STEP = TM_SEQ // GRID_W
KV_WINDOW_ROWS = 2 * ROWS_PER_STEP


def _na_step_geometry(s):
    steps_per_prompt = SEQ // TM_SEQ
    n_prompt_steps = T_PROMPT // TM_SEQ
    is_prompt = s < n_prompt_steps
    jj = jnp.where(is_prompt, lax.rem(s, steps_per_prompt), s - n_prompt_steps)
    rows = jnp.where(is_prompt, SEQ // GRID_W, DEC_SEQ // GRID_W)
    seq_row0 = jnp.where(is_prompt, (s // steps_per_prompt) * (SEQ // GRID_W), T_PROMPT // GRID_W)
    return jj * ROWS_PER_STEP, rows, seq_row0


def _na_window_row(s):
    r0, rows, seq_row0 = _na_step_geometry(s)
    return jnp.clip(r0 - NA_ROWS // 2, 0, rows - KV_WINDOW_ROWS), seq_row0


def _na_kernel(q_ref, k_ref, v_ref, bias_ref, o_ref):
    s = pl.program_id(0)
    r0, rows, _ = _na_step_geometry(s)
    win0, _ = _na_window_row(s)
    gl = HEADS_PER_GROUP * NA_HEAD_DIM
    lane = lax.broadcasted_iota(jnp.int32, (GRID_W, gl), 1)
    n_keys = NA_ROWS * GRID_W

    def row_body(i, carry):
        r = r0 + i
        row_start = jnp.clip(r - NA_ROWS // 2, 0, rows - NA_ROWS)
        k0 = pl.multiple_of((row_start - win0) * GRID_W, GRID_W)
        d0 = NA_ROWS - 1 + row_start - r
        q0 = pl.multiple_of(i * GRID_W, GRID_W)
        for p in range(NA_HEADS // HEADS_PER_GROUP):
            ls = slice(p * gl, (p + 1) * gl)
            q2 = q_ref[pl.ds(q0, GRID_W), ls] * jnp.asarray(NA_HEAD_DIM ** -0.5, BF16)
            zq = jnp.zeros_like(q2)
            qs = jnp.concatenate(
                [jnp.where((lane >= h * NA_HEAD_DIM) & (lane < (h + 1) * NA_HEAD_DIM), q2, zq)
                 for h in range(HEADS_PER_GROUP)], axis=0)
            kw = k_ref[pl.ds(k0, n_keys), ls]
            vw = v_ref[pl.ds(k0, n_keys), ls]
            sc = _dot_nt(qs, kw)
            sc = jnp.concatenate(
                [sc[:, j * LANES:(j + 1) * LANES] + bias_ref[p, d0 + 2 * j] for j in range(n_keys // LANES)],
                axis=1)
            m = jnp.max(sc, axis=-1, keepdims=True)
            e = jnp.exp(sc - m)
            den = jnp.sum(e, axis=-1, keepdims=True)
            o = _dot(e.astype(BF16), vw) / den
            out = o[0:GRID_W]
            for h in range(1, HEADS_PER_GROUP):
                out = jnp.where(lane >= h * NA_HEAD_DIM, o[h * GRID_W:(h + 1) * GRID_W], out)
            o_ref[pl.ds(q0, GRID_W), ls] = out.astype(BF16)
        return carry

    lax.fori_loop(0, ROWS_PER_STEP, row_body, 0, unroll=8)


def _na_bias_table(rpb):
    qc = np.arange(GRID_W)[:, None]
    kc = np.arange(GRID_W)[None, :]
    dc = np.clip(kc - qc + NA_COLS - 1, 0, 2 * NA_COLS - 2)
    col_start = np.clip(qc - NA_COLS // 2, 0, GRID_W - NA_COLS)
    mask = (kc - col_start >= 0) & (kc - col_start < NA_COLS)
    tab = jnp.where(mask[None, None], rpb[:, :, dc].astype(F32), NEG_INF)
    n_dr = 2 * NA_ROWS - 1
    ng = NA_HEADS // HEADS_PER_GROUP
    tab = tab.reshape(ng, HEADS_PER_GROUP, n_dr, GRID_W, GRID_W).transpose(0, 2, 1, 3, 4)
    tab = tab.reshape(ng, n_dr, HEADS_PER_GROUP * GRID_W, GRID_W)
    return jnp.concatenate([tab[:, :-1], tab[:, 1:]], axis=-1)


def _neighbourhood_attention(u, rpb):
    def kv_spec(col_block):
        def index(s):
            win0, seq_row0 = _na_window_row(s)
            return ((seq_row0 + win0) * GRID_W, col_block * D_NA)
        return pl.BlockSpec((pl.Element(KV_WINDOW_ROWS * GRID_W), pl.Element(D_NA)), index)

    bias = _na_bias_table(rpb)
    return pl.pallas_call(
        _na_kernel,
        grid=(T_ALL // TM_SEQ,),
        in_specs=[pl.BlockSpec((TM_SEQ, D_NA), lambda s: (s, 1)),
                  kv_spec(2), kv_spec(3), _const_spec(bias.shape)],
        out_specs=pl.BlockSpec((TM_SEQ, D_NA), lambda s: (s, 0)),
        out_shape=jax.ShapeDtypeStruct((T_ALL, D_NA), BF16),
        compiler_params=_cparams(), name="neighbourhood_attention")(u, u, u, bias)


def _mix_ln_kernel(ya_ref, yb_ref, xp_ref, xs_ref, w_ref, g_ref, b_ref, o_ref, *, n_prompt_tiles):
    i = pl.program_id(0)
    for rs in _sub_tiles(TM_ROW):
        x = jnp.where(i < n_prompt_tiles, xp_ref[rs, :], xs_ref[rs, :])
        mix = _dot(ya_ref[rs, :], w_ref[:D_POOL, :]) + _dot(yb_ref[rs, :], w_ref[D_POOL:, :])
        o_ref[rs, :] = _layer_norm(DEEPNORM_ALPHA * x + mix, g_ref[...], b_ref[...])


def _mix_ln(ya, yb, xp, xs, w_out, g, b):
    npt = T_PROMPT // TM_ROW
    spec_p, spec_s = _two_group_specs(TM_ROW, npt)
    return pl.pallas_call(
        functools.partial(_mix_ln_kernel, n_prompt_tiles=npt),
        grid=(T_ALL // TM_ROW,),
        in_specs=[pl.BlockSpec((TM_ROW, D_POOL), lambda i: (i, 0)),
                  pl.BlockSpec((TM_ROW, D_NA), lambda i: (i, 0)),
                  spec_p, spec_s,
                  _const_spec((D_MODEL, D_MODEL)), _const_spec((1, D_MODEL)), _const_spec((1, D_MODEL))],
        out_specs=pl.BlockSpec((TM_ROW, D_MODEL), lambda i: (i, 0)),
        out_shape=jax.ShapeDtypeStruct((T_ALL, D_MODEL), F32),
        compiler_params=_cparams(), name="mix_ln1")(ya, yb, xp, xs, w_out, g, b)


def _kv_kernel(m_ref, w_ref, o_ref):
    o_ref[...] = _dot(m_ref[...], w_ref[...]).astype(BF16)


def _memory_kv(mem, w_xkv):
    n_mem_rows = mem.shape[0]
    tn = 1024
    return pl.pallas_call(
        _kv_kernel,
        grid=(2 * D_MODEL // tn,),
        in_specs=[_const_spec((n_mem_rows, D_MODEL)), pl.BlockSpec((D_MODEL, tn), lambda j: (0, j))],
        out_specs=pl.BlockSpec((n_mem_rows, tn), lambda j: (0, j)),
        out_shape=jax.ShapeDtypeStruct((n_mem_rows, 2 * D_MODEL), BF16),
        compiler_params=_cparams(), name="memory_kv")(mem, w_xkv)


def _xattn_kernel(x_ref, wq_ref, k_ref, v_ref, o_ref):
    for rs in _sub_tiles(TM_ROW):
        q = _dot(x_ref[rs, :].astype(BF16), wq_ref[...]).astype(BF16)
        for h in range(XA_HEADS):
            sl = slice(h * XA_HEAD_DIM, (h + 1) * XA_HEAD_DIM)
            sc = _dot_nt(q[:, sl], k_ref[:, sl]) * (XA_HEAD_DIM ** -0.5)
            m = jnp.max(sc, axis=-1, keepdims=True)
            e = jnp.exp(sc - m)
            den = jnp.sum(e, axis=-1, keepdims=True)
            o_ref[rs, sl] = (_dot(e.astype(BF16), v_ref[:, sl]) / den).astype(BF16)


def _mem_batch(i):
    return jnp.minimum((i * TM_ROW) // SEQ, BATCH)


def _cross_attention(x1, w_xq, kv):
    return pl.pallas_call(
        _xattn_kernel,
        grid=(T_ALL // TM_ROW,),
        in_specs=[pl.BlockSpec((TM_ROW, D_MODEL), lambda i: (i, 0)),
                  _const_spec((D_MODEL, D_MODEL)),
                  pl.BlockSpec((N_MEM, D_MODEL), lambda i: (_mem_batch(i), 0)),
                  pl.BlockSpec((N_MEM, D_MODEL), lambda i: (_mem_batch(i), 1))],
        out_specs=pl.BlockSpec((TM_ROW, D_MODEL), lambda i: (i, 0)),
        out_shape=jax.ShapeDtypeStruct((T_ALL, D_MODEL), BF16),
        compiler_params=_cparams(), name="cross_attention")(x1, w_xq, kv, kv)


def _route_kernel(a_ref, x1_ref, wo_ref, g_ref, b_ref, wr_ref, br_ref,
                  x2_ref, x2p_ref, idx_ref, gate_ref, rank_ref, cnt_ref, carry_ref):
    @pl.when(pl.program_id(0) == 0)
    def _():
        carry_ref[...] = jnp.zeros_like(carry_ref)

    projs = [_dot(a_ref[rs, :], wo_ref[...]) for rs in _sub_tiles(TM_ROW)]
    for rs, proj in zip(_sub_tiles(TM_ROW), projs):
        _route_sub_tile(rs, proj, x1_ref, g_ref, b_ref, wr_ref, br_ref,
                        x2_ref, x2p_ref, idx_ref, gate_ref, rank_ref, carry_ref)
    cnt_ref[...] = carry_ref[...]


def _route_sub_tile(rs, proj, x1_ref, g_ref, b_ref, wr_ref, br_ref,
                    x2_ref, x2p_ref, idx_ref, gate_ref, rank_ref, carry_ref):
    n = SUB_ROWS
    x2 = _layer_norm(DEEPNORM_ALPHA * x1_ref[rs, :] + proj, g_ref[...], b_ref[...])
    x2_ref[rs, :] = x2
    half = D_MODEL // 2
    packed = pltpu.pack_elementwise([x2[:, :half], x2[:, half:]], packed_dtype=BF16)
    for s in range(ROW_TILE):
        x2p_ref[pl.ds(rs.start * ROW_TILE + s, n, stride=ROW_TILE), :] = packed[:, s * LANES:(s + 1) * LANES]

    lane = lax.broadcasted_iota(jnp.int32, (n, LANES), 1)
    lanef = lane.astype(F32)
    logits = jnp.where(lane < N_EXPERTS, _dot(x2.astype(BF16), wr_ref[...]) + br_ref[...], -jnp.inf)
    top_v, top_i, hots = [], [], []
    for _ in range(TOP_K):
        m = jnp.max(logits, axis=-1, keepdims=True)
        first = jnp.min(jnp.where(logits == m, lanef, float(LANES)), axis=-1, keepdims=True)
        hot = lanef == first
        top_v.append(m)
        top_i.append(first)
        hots.append(hot)
        logits = jnp.where(hot, -jnp.inf, logits)
    ex = [jnp.exp(v - top_v[0]) for v in top_v]
    den = ex[0] + ex[1] + ex[2] + ex[3]

    chosen = jnp.zeros((n, LANES), F32)
    for hot in hots:
        chosen = jnp.where(hot, 1.0, chosen)
    r_i = lax.broadcasted_iota(jnp.int32, (n, n), 0)
    c_i = lax.broadcasted_iota(jnp.int32, (n, n), 1)
    before = jnp.where(c_i < r_i, 1.0, 0.0).astype(BF16)
    base = carry_ref[...] + _dot(before, chosen.astype(BF16))

    idx_out = jnp.zeros((n, LANES), jnp.int32)
    gate_out = jnp.zeros((n, LANES), F32)
    rank_out = jnp.zeros((n, LANES), jnp.int32)
    for k in range(TOP_K):
        rank_k = jnp.sum(jnp.where(hots[k], base, 0.0), axis=-1, keepdims=True)
        idx_out = jnp.where(lane == k, top_i[k].astype(jnp.int32), idx_out)
        gate_out = jnp.where(lane == k, ex[k] / den, gate_out)
        rank_out = jnp.where(lane == k, rank_k.astype(jnp.int32), rank_out)
    idx_ref[rs, :] = idx_out[:, :ROUTE_COLS]
    gate_ref[rs, :] = gate_out[:, :ROUTE_COLS]
    rank_ref[rs, :] = rank_out[:, :ROUTE_COLS]
    carry_ref[...] = carry_ref[...] + jnp.sum(chosen, axis=0, keepdims=True)


def _project_norm_route(attn, x1, w_xo, g, b, w_router, b_router):
    row = lambda w: pl.BlockSpec((TM_ROW, w), lambda i: (i, 0))
    return pl.pallas_call(
        _route_kernel,
        grid=(T_ALL // TM_ROW,),
        in_specs=[row(D_MODEL), row(D_MODEL), _const_spec((D_MODEL, D_MODEL)),
                  _const_spec((1, D_MODEL)), _const_spec((1, D_MODEL)),
                  _const_spec((D_MODEL, LANES)), _const_spec((1, LANES))],
        out_specs=[row(D_MODEL), pl.BlockSpec((TM_ROW * ROW_TILE, LANES), lambda i: (i, 0)),
                   row(ROUTE_COLS), row(ROUTE_COLS), row(ROUTE_COLS),
                   pl.BlockSpec((1, LANES), lambda i: (0, 0))],
        out_shape=[jax.ShapeDtypeStruct((T_ALL, D_MODEL), F32),
                   jax.ShapeDtypeStruct((T_ALL * ROW_TILE, LANES), jnp.uint32),
                   jax.ShapeDtypeStruct((T_ALL, ROUTE_COLS), jnp.int32),
                   jax.ShapeDtypeStruct((T_ALL, ROUTE_COLS), F32),
                   jax.ShapeDtypeStruct((T_ALL, ROUTE_COLS), jnp.int32),
                   jax.ShapeDtypeStruct((1, LANES), F32)],
        scratch_shapes=[pltpu.VMEM((1, LANES), F32)],
        compiler_params=_cparams(), name="proj_ln2_route")(attn, x1, w_xo, g, b, w_router, b_router)


def _gather_kernel(tok_ref, x2p_ref, o_ref, sem):
    def issue(q, carry):
        for r in range(DMA_GROUP):
            j = q * DMA_GROUP + r
            src = pl.multiple_of(tok_ref[j] * ROW_TILE, ROW_TILE)
            dst = pl.multiple_of(j * ROW_TILE, ROW_TILE)
            pltpu.make_async_copy(x2p_ref.at[pl.ds(src, ROW_TILE)], o_ref.at[pl.ds(dst, ROW_TILE)],
                                  sem).start(priority=r % 2)
        return carry

    lax.fori_loop(0, TS_GATHER // DMA_GROUP, issue, 0)
    pltpu.make_async_copy(x2p_ref.at[pl.ds(0, TS_GATHER * ROW_TILE)], o_ref, sem).wait()


def _gather_rows(slot_tok, x2p):
    return pl.pallas_call(
        _gather_kernel,
        grid=(N_SLOTS // TS_GATHER,),
        in_specs=[pl.BlockSpec((TS_GATHER,), lambda i: (i,), memory_space=pltpu.SMEM),
                  pl.BlockSpec(memory_space=pl.ANY)],
        out_specs=pl.BlockSpec((TS_GATHER * ROW_TILE, LANES), lambda i: (i, 0)),
        out_shape=jax.ShapeDtypeStruct((N_SLOTS * ROW_TILE, LANES), jnp.uint32),
        scratch_shapes=[pltpu.SemaphoreType.DMA],
        compiler_params=_cparams(), name="gather_rows")(slot_tok, x2p)


def _unpack_rows(x_ref, rs):
    lo, hi = [], []
    for s in range(ROW_TILE):
        slab = x_ref[pl.ds(rs.start * ROW_TILE + s, rs.stop - rs.start, stride=ROW_TILE), :]
        lo.append(pltpu.unpack_elementwise(slab, index=0, packed_dtype=BF16, unpacked_dtype=F32).astype(BF16))
        hi.append(pltpu.unpack_elementwise(slab, index=1, packed_dtype=BF16, unpacked_dtype=F32).astype(BF16))
    return jnp.concatenate(lo, axis=1), jnp.concatenate(hi, axis=1)


STEP_VALID = 1
STEP_NEW_TILE = 2


def _gate_up_kernel(blk_ref, exp_ref, wcol_ref, ocol_ref, flag_ref, x_ref, wg_ref, wu_ref, bg_ref, bu_ref,
                    o_ref, wg_bf_ref, wu_bf_ref):
    flags = flag_ref[pl.program_id(0)]

    @pl.when((flags & STEP_NEW_TILE) != 0)
    def _():
        wg_bf_ref[...] = wg_ref[...].astype(BF16)
        wu_bf_ref[...] = wu_ref[...].astype(BF16)

    @pl.when((flags & STEP_VALID) != 0)
    def _():
        half = D_MODEL // 2
        x_lo, x_hi = _unpack_rows(x_ref, slice(0, TM_EXP))
        gate = _dot(x_lo, wg_bf_ref[:half, :]) + _dot(x_hi, wg_bf_ref[half:, :]) + bg_ref[0:1, :]
        up = _dot(x_lo, wu_bf_ref[:half, :]) + _dot(x_hi, wu_bf_ref[half:, :]) + bu_ref[0:1, :]
        gate = jnp.minimum(gate, SWIGLU_LIMIT)
        up = jnp.clip(up, -SWIGLU_LIMIT, SWIGLU_LIMIT)
        o_ref[...] = (gate * jax.nn.sigmoid(SWIGLU_ALPHA * gate) * (up + 1.0)).astype(BF16)

    @pl.when((flags & STEP_VALID) == 0)
    def _():
        o_ref[...] = jnp.zeros_like(o_ref)


def _down_kernel(blk_ref, exp_ref, wcol_ref, ocol_ref, flag_ref, a_ref, w_ref, b_ref, o_ref, w_bf_ref):
    flags = flag_ref[pl.program_id(0)]

    @pl.when((flags & STEP_NEW_TILE) != 0)
    def _():
        w_bf_ref[...] = w_ref[...].astype(BF16)

    @pl.when((flags & STEP_VALID) != 0)
    def _():
        o_ref[...] = _dot(a_ref[...], w_bf_ref[...]) + b_ref[0:1, :]

    @pl.when((flags & STEP_VALID) == 0)
    def _():
        o_ref[...] = jnp.zeros_like(o_ref)


def _pick(table, index):
    hot = index[..., None] == jnp.arange(table.shape[0], dtype=index.dtype)
    return jnp.sum(jnp.where(hot, table, 0), axis=-1)


def _expert_schedule(blocks_per_expert, n_col_tiles):
    n_steps = N_BLOCKS * n_col_tiles
    blk_end = jnp.cumsum(blocks_per_expert)
    blk_start = blk_end - blocks_per_expert
    total = blk_end[-1] * n_col_tiles
    step = jnp.arange(n_steps, dtype=jnp.int32)
    valid = step < total
    s = jnp.minimum(step, total - 1)
    e = jnp.sum((s[:, None] >= (blk_end * n_col_tiles)[None, :]).astype(jnp.int32), axis=1)
    start = _pick(blk_start, e)
    nb = _pick(blocks_per_expert, e)
    local = s - start * n_col_tiles
    col = local // nb
    blk = start + local - col * nb
    tile_id = e * n_col_tiles + col
    new_tile = valid & jnp.concatenate([jnp.ones((1,), bool), tile_id[1:] != tile_id[:-1]])
    flags = valid.astype(jnp.int32) * STEP_VALID + new_tile.astype(jnp.int32) * STEP_NEW_TILE
    spare = step - total
    blk = jnp.where(valid, blk, blk_end[-1] + spare // n_col_tiles)
    out_col = jnp.where(valid, col, spare % n_col_tiles)
    i32 = lambda a: a.astype(jnp.int32)
    return i32(blk), i32(e), i32(col), i32(out_col), flags


def _experts(xs, blocks_per_expert, w_gu, b_gu, w_down, b_down):
    rows = lambda s, blk, ex, wc, oc, fl: (blk[s], 0)
    out = lambda s, blk, ex, wc, oc, fl: (blk[s], oc[s])

    def weight(col_offset):
        return lambda s, blk, ex, wc, oc, fl: (ex[s], 0, col_offset + wc[s])

    n_gu = D_EXPERT // TN_GU
    sched = _expert_schedule(blocks_per_expert, n_gu)
    act = pl.pallas_call(
        _gate_up_kernel,
        grid_spec=pltpu.PrefetchScalarGridSpec(
            num_scalar_prefetch=5, grid=(N_BLOCKS * n_gu,),
            in_specs=[
                pl.BlockSpec((TM_EXP * ROW_TILE, LANES), rows),
                pl.BlockSpec((None, D_MODEL, TN_GU), weight(0)),
                pl.BlockSpec((None, D_MODEL, TN_GU), weight(n_gu)),
                pl.BlockSpec((None, SUBLANES, TN_GU), weight(0)),
                pl.BlockSpec((None, SUBLANES, TN_GU), weight(n_gu)),
            ],
            out_specs=pl.BlockSpec((TM_EXP, TN_GU), out),
            scratch_shapes=[pltpu.VMEM((D_MODEL, TN_GU), BF16), pltpu.VMEM((D_MODEL, TN_GU), BF16)]),
        out_shape=jax.ShapeDtypeStruct((N_SLOTS, D_EXPERT), BF16),
        compiler_params=_cparams(), name="expert_gate_up")(*sched, xs, w_gu, w_gu, b_gu, b_gu)

    n_dn = D_MODEL // TN_DOWN
    sched = _expert_schedule(blocks_per_expert, n_dn)
    return pl.pallas_call(
        _down_kernel,
        grid_spec=pltpu.PrefetchScalarGridSpec(
            num_scalar_prefetch=5, grid=(N_BLOCKS * n_dn,),
            in_specs=[
                pl.BlockSpec((TM_EXP, D_EXPERT), rows),
                pl.BlockSpec((None, D_EXPERT, TN_DOWN), weight(0)),
                pl.BlockSpec((None, SUBLANES, TN_DOWN), weight(0)),
            ],
            out_specs=pl.BlockSpec((TM_EXP, TN_DOWN), out),
            scratch_shapes=[pltpu.VMEM((D_EXPERT, TN_DOWN), BF16)]),
        out_shape=jax.ShapeDtypeStruct((N_SLOTS, D_MODEL), F32),
        compiler_params=_cparams(), name="expert_down")(*sched, act, w_down, b_down)


def _combine_kernel(dest_ref, gate_ref, x2_ref, y_ref, g_ref, b_ref, o_ref, buf_ref, sem):
    def issue(q, carry):
        for r in range(SUBLANES):
            t = q * SUBLANES + r
            for k in range(TOP_K):
                pltpu.make_async_copy(y_ref.at[pl.ds(dest_ref[t * TOP_K + k], 1)],
                                      buf_ref.at[k, q, pl.ds(r, 1)], sem).start(priority=k % 2)
        return carry

    n_tiles = TT_COMB // SUBLANES
    lax.fori_loop(0, n_tiles, issue, 0)
    for k in range(TOP_K):
        for q in range(n_tiles):
            pltpu.make_async_copy(y_ref.at[pl.ds(0, SUBLANES)], buf_ref.at[k, q], sem).wait()
    for rs in _sub_tiles(TT_COMB):
        gates = gate_ref[rs, :]
        tiles = slice(rs.start // SUBLANES, rs.stop // SUBLANES)
        plane = lambda k: buf_ref[k, tiles].reshape(SUB_ROWS, D_MODEL)
        h = gates[:, 0:1] * plane(0)
        for k in range(1, TOP_K):
            h = h + gates[:, k:k + 1] * plane(k)
        o_ref[rs, :] = _layer_norm(DEEPNORM_ALPHA * x2_ref[rs, :] + h, g_ref[...], b_ref[...])


def _combine(dest_flat, gates, x2, y_slots, g, b, first_tile, n_tiles):
    n = TT_COMB * TOP_K
    return pl.pallas_call(
        _combine_kernel,
        grid=(n_tiles,),
        in_specs=[pl.BlockSpec((n,), lambda i: (i + first_tile,), memory_space=pltpu.SMEM),
                  pl.BlockSpec((TT_COMB, ROUTE_COLS), lambda i: (i + first_tile, 0)),
                  pl.BlockSpec((TT_COMB, D_MODEL), lambda i: (i + first_tile, 0)),
                  pl.BlockSpec(memory_space=pl.ANY),
                  _const_spec((1, D_MODEL)), _const_spec((1, D_MODEL))],
        out_specs=pl.BlockSpec((TT_COMB, D_MODEL), lambda i: (i, 0)),
        out_shape=jax.ShapeDtypeStruct((n_tiles * TT_COMB, D_MODEL), F32),
        scratch_shapes=[pltpu.VMEM((TOP_K, TT_COMB // SUBLANES, SUBLANES, D_MODEL), F32),
                        pltpu.SemaphoreType.DMA],
        compiler_params=_cparams(), name="combine_ln3")(dest_flat, gates, x2, y_slots, g, b)


def _slot_tokens(top_i, counts, blocks_per_expert):
    n_assign = T_ALL * TOP_K
    flat = jnp.arange(n_assign, dtype=jnp.int32)
    _, order = lax.sort((top_i.reshape(n_assign), flat), num_keys=1, is_stable=True)
    sorted_tok = order // TOP_K
    blk_end = jnp.cumsum(blocks_per_expert)
    group_start = jnp.cumsum(counts) - counts
    blk = jnp.arange(N_BLOCKS, dtype=jnp.int32)
    e_blk = jnp.minimum(jnp.sum((blk[:, None] >= blk_end[None, :]).astype(jnp.int32), axis=1), N_EXPERTS - 1)
    first_blk = _pick(blk_end - blocks_per_expert, e_blk)
    per_slot = lambda v: jnp.repeat(v, TM_EXP)
    pos = per_slot((blk - first_blk) * TM_EXP) + jnp.tile(jnp.arange(TM_EXP, dtype=jnp.int32), N_BLOCKS)
    valid = per_slot(blk < blk_end[-1]) & (pos < per_slot(_pick(counts, e_blk)))
    src = jnp.clip(per_slot(_pick(group_start, e_blk)) + pos, 0, n_assign - 1)
    spread = jnp.arange(N_SLOTS, dtype=jnp.int32) % T_ALL
    return jnp.where(valid, sorted_tok[src], spread)


def kernel(x_prompt, x_sample, mem_prompt, mem_sample, w_in, w_pool, pool_scale, rpb, w_out,
           ln1_g, ln1_b, w_xq, w_xkv, w_xo, ln2_g, ln2_b,
           w_router, b_router, w_gu, b_gu, w_down, b_down, ln3_g, ln3_b):
    assert w_in.shape[0] == 1, "single-layer problem"
    xp = x_prompt.reshape(T_PROMPT, D_MODEL)
    xs = x_sample.reshape(T_SAMPLE, D_MODEL)
    mem = jnp.concatenate([mem_prompt.reshape(BATCH * N_MEM, D_MODEL),
                           mem_sample.reshape(N_MEM, D_MODEL)], axis=0).astype(BF16)
    row = lambda v: v.reshape(1, -1).astype(F32)

    u = _inproj(xp, xs, w_in[0].astype(BF16))
    ya = _pool_mixer(u, w_pool[0].astype(BF16), row(pool_scale[0]))
    yb = _neighbourhood_attention(u, rpb[0])
    x1 = _mix_ln(ya, yb, xp, xs, w_out[0].astype(BF16), row(ln1_g[0]), row(ln1_b[0]))

    kv = _memory_kv(mem, w_xkv[0].astype(BF16))
    attn = _cross_attention(x1, w_xq[0].astype(BF16), kv)
    w_r = jnp.pad(w_router[0], ((0, 0), (0, LANES - N_EXPERTS))).astype(BF16)
    b_r = jnp.pad(b_router[0].astype(F32), (0, LANES - N_EXPERTS)).reshape(1, LANES)
    x2, x2p, top_i, gates, rank, counts = _project_norm_route(
        attn, x1, w_xo[0].astype(BF16), row(ln2_g[0]), row(ln2_b[0]), w_r, b_r)

    counts = counts[0, :N_EXPERTS].astype(jnp.int32)
    blocks_per_expert = (counts + TM_EXP - 1) // TM_EXP
    blk_end = jnp.cumsum(blocks_per_expert)
    slot_start = (blk_end - blocks_per_expert) * TM_EXP
    dest = (_pick(slot_start, top_i[:, :TOP_K]) + rank[:, :TOP_K]).reshape(T_ALL * TOP_K)
    slot_tok = _slot_tokens(top_i[:, :TOP_K], counts, blocks_per_expert)

    xs_slots = _gather_rows(slot_tok, x2p)
    tiled = lambda bias: jnp.broadcast_to(bias[:, None, :], (N_EXPERTS, SUBLANES, bias.shape[-1]))
    y_slots = _experts(xs_slots, blocks_per_expert, w_gu[0], tiled(b_gu[0]), w_down[0], tiled(b_down[0]))

    g3, b3 = row(ln3_g[0]), row(ln3_b[0])
    n_p = T_PROMPT // TT_COMB
    y_prompt = _combine(dest, gates, x2, y_slots, g3, b3, 0, n_p)
    y_sample = _combine(dest, gates, x2, y_slots, g3, b3, n_p, T_SAMPLE // TT_COMB)
    return (y_prompt.reshape(BATCH, SEQ, D_MODEL), y_sample.reshape(1, DEC_SEQ, D_MODEL))
```

```python
import functools

import numpy as np
import jax
import jax.numpy as jnp
from jax import lax
from jax.experimental import pallas as pl
from jax.experimental.pallas import tpu as pltpu

F32 = jnp.float32
BF16 = jnp.bfloat16

D_MODEL = 2048
BATCH, SEQ = 4, 4096
DEC_SEQ = 8192
T_PROMPT = BATCH * SEQ
T_SAMPLE = DEC_SEQ
T_ALL = T_PROMPT + T_SAMPLE
GRID_W = 64
D_POOL = 1024
POOL_WINDOWS = (2, 4, 8, 16)
POOL_GROUP = 256
D_NA = 1024
NA_HEADS = 16
NA_HEAD_DIM = 64
NA_ROWS = 8
NA_COLS = 16
D_IN = D_POOL + 3 * D_NA
N_MEM = 256
XA_HEADS = 4
XA_HEAD_DIM = 512
N_EXPERTS = 32
TOP_K = 4
D_EXPERT = 2048
SWIGLU_LIMIT = 7.0
SWIGLU_ALPHA = 1.702
LN_EPS = 1e-5
DEEPNORM_ALPHA = 2.0 ** 0.25
NEG_INF = -1e30

LANES = 128
SUBLANES = 8
ROW_TILE = D_MODEL // 2 // LANES
DMA_GROUP = 8
VMEM_LIMIT = 56 * 1024 * 1024
TM_IN = 512
TM_SEQ = 512
POOL_HALO = 16
POOL_CHUNK = 128
TM_ROW = 512
SUB_ROWS = 256
TM_EXP = 512
TN_GU = 1024
TN_DOWN = 1024
TS_GATHER = 4096
TT_COMB = 512
HEADS_PER_GROUP = 4
N_SLOTS = T_ALL * TOP_K + N_EXPERTS * TM_EXP
N_BLOCKS = N_SLOTS // TM_EXP


def _cparams(n_axes=1):
    return pltpu.CompilerParams(dimension_semantics=("arbitrary",) * n_axes,
                                vmem_limit_bytes=VMEM_LIMIT)


def _dot(a, b):
    return jnp.dot(a, b, preferred_element_type=F32)


def _dot_nt(a, b):
    return lax.dot_general(a, b, (((1,), (1,)), ((), ())), preferred_element_type=F32)


def _layer_norm(x, g, b):
    mu = jnp.mean(x, axis=-1, keepdims=True)
    xc = x - mu
    var = jnp.mean(xc * xc, axis=-1, keepdims=True)
    return xc * lax.rsqrt(var + LN_EPS) * g + b


def _sub_tiles(n_rows):
    return [slice(r, r + SUB_ROWS) for r in range(0, n_rows, SUB_ROWS)]


def _const_spec(shape):
    nd = len(shape)
    return pl.BlockSpec(shape, lambda *_: (0,) * nd, pipeline_mode=pl.Buffered(1))


def _inproj_kernel(xp_ref, xs_ref, w_ref, o_ref, *, n_prompt_tiles):
    i = pl.program_id(0)
    x = jnp.where(i < n_prompt_tiles, xp_ref[...], xs_ref[...]).astype(BF16)
    for c in range(D_IN // 1024):
        sl = slice(c * 1024, (c + 1) * 1024)
        o_ref[:, sl] = _dot(x, w_ref[:, sl]).astype(BF16)


def _two_group_specs(tm, n_prompt_tiles):
    last = n_prompt_tiles - 1
    return (pl.BlockSpec((tm, D_MODEL), lambda i: (jnp.minimum(i, last), 0)),
            pl.BlockSpec((tm, D_MODEL), lambda i: (jnp.maximum(i - n_prompt_tiles, 0), 0)))


def _inproj(xp, xs, w_in):
    npt = T_PROMPT // TM_IN
    spec_p, spec_s = _two_group_specs(TM_IN, npt)
    return pl.pallas_call(
        functools.partial(_inproj_kernel, n_prompt_tiles=npt),
        grid=(T_ALL // TM_IN,),
        in_specs=[spec_p, spec_s, _const_spec((D_MODEL, D_IN))],
        out_specs=pl.BlockSpec((TM_IN, D_IN), lambda i: (i, 0)),
        out_shape=jax.ShapeDtypeStruct((T_ALL, D_IN), BF16),
        compiler_params=_cparams(), name="inproj")(xp, xs, w_in)


def _seq_position(i):
    per_prompt = SEQ // TM_SEQ
    n_prompt = T_PROMPT // TM_SEQ
    is_prompt = i < n_prompt
    pos = jnp.where(is_prompt, lax.rem(i, per_prompt), i - n_prompt)
    nblk = jnp.where(is_prompt, per_prompt, DEC_SEQ // TM_SEQ)
    return pos, nblk


def _pool_kernel(u_ref, prev_ref, next_ref, a_ref, wp_ref, sc_ref, o_ref):
    pos, nblk = _seq_position(pl.program_id(0))
    cur = u_ref[...]
    zero = jnp.zeros((POOL_HALO, D_POOL), BF16)
    prev = jnp.where(pos == 0, zero, prev_ref[...])
    nxt = jnp.where(pos == nblk - 1, zero, next_ref[...])
    ext = jnp.concatenate([prev, cur, nxt], axis=0)
    t = pos * TM_SEQ + lax.broadcasted_iota(jnp.int32, (TM_SEQ, 1), 0)
    seq_len = nblk * TM_SEQ
    for g, w in enumerate(POOL_WINDOWS):
        sl = slice(g * POOL_GROUP, (g + 1) * POOL_GROUP)
        win_sum = jnp.concatenate(
            [_dot(a_ref[g, c:c + POOL_CHUNK, c:c + POOL_CHUNK + 2 * POOL_HALO],
                  ext[c:c + POOL_CHUNK + 2 * POOL_HALO, sl]) for c in range(0, TM_SEQ, POOL_CHUNK)], axis=0)
        cnt = (jnp.minimum(t - w // 2 + w, seq_len) - jnp.maximum(t - w // 2, 0)).astype(F32)
        p = win_sum / cnt - cur[:, sl].astype(F32)
        y = _dot(p.astype(BF16), wp_ref[g]) * sc_ref[:, sl]
        o_ref[:, sl] = y.astype(BF16)


def _pool_band_matrices():
    t = np.arange(TM_SEQ)[:, None]
    c = np.arange(TM_SEQ + 2 * POOL_HALO)[None, :] - POOL_HALO
    mats = [((c - t >= -(w // 2)) & (c - t <= w // 2 - 1)) for w in POOL_WINDOWS]
    return jnp.asarray(np.stack(mats).astype(np.float32), dtype=BF16)


def _pool_mixer(u, w_pool, pool_scale):
    per_halo = TM_SEQ // POOL_HALO
    n_halo = T_ALL // POOL_HALO
    return pl.pallas_call(
        _pool_kernel,
        grid=(T_ALL // TM_SEQ,),
        in_specs=[
            pl.BlockSpec((TM_SEQ, D_POOL), lambda i: (i, 0)),
            pl.BlockSpec((POOL_HALO, D_POOL), lambda i: (jnp.maximum(i * per_halo - 1, 0), 0)),
            pl.BlockSpec((POOL_HALO, D_POOL), lambda i: (jnp.minimum((i + 1) * per_halo, n_halo - 1), 0)),
            _const_spec((len(POOL_WINDOWS), TM_SEQ, TM_SEQ + 2 * POOL_HALO)),
            _const_spec((len(POOL_WINDOWS), POOL_GROUP, POOL_GROUP)),
            _const_spec((1, D_POOL)),
        ],
        out_specs=pl.BlockSpec((TM_SEQ, D_POOL), lambda i: (i, 0)),
        out_shape=jax.ShapeDtypeStruct((T_ALL, D_POOL), BF16),
        compiler_params=_cparams(), name="pool_mixer")(
            u, u, u, _pool_band_matrices(), w_pool, pool_scale)


ROWS_PER_STEP = TM_SEQ // GRID_W
KV_WINDOW_ROWS = 2 * ROWS_PER_STEP


def _na_step_geometry(s):
    steps_per_prompt = SEQ // TM_SEQ
    n_prompt_steps = T_PROMPT // TM_SEQ
    is_prompt = s < n_prompt_steps
    jj = jnp.where(is_prompt, lax.rem(s, steps_per_prompt), s - n_prompt_steps)
    rows = jnp.where(is_prompt, SEQ // GRID_W, DEC_SEQ // GRID_W)
    seq_row0 = jnp.where(is_prompt, (s // steps_per_prompt) * (SEQ // GRID_W), T_PROMPT // GRID_W)
    return jj * ROWS_PER_STEP, rows, seq_row0


def _na_window_row(s):
    r0, rows, seq_row0 = _na_step_geometry(s)
    return jnp.clip(r0 - NA_ROWS // 2, 0, rows - KV_WINDOW_ROWS), seq_row0


def _na_kernel(q_ref, k_ref, v_ref, bias_ref, o_ref):
    s = pl.program_id(0)
    r0, rows, _ = _na_step_geometry(s)
    win0, _ = _na_window_row(s)
    gl = HEADS_PER_GROUP * NA_HEAD_DIM
    lane = lax.broadcasted_iota(jnp.int32, (GRID_W, gl), 1)
    n_keys = NA_ROWS * GRID_W

    def row_body(i, carry):
        r = r0 + i
        row_start = jnp.clip(r - NA_ROWS // 2, 0, rows - NA_ROWS)
        k0 = pl.multiple_of((row_start - win0) * GRID_W, GRID_W)
        d0 = NA_ROWS - 1 + row_start - r
        q0 = pl.multiple_of(i * GRID_W, GRID_W)
        for p in range(NA_HEADS // HEADS_PER_GROUP):
            ls = slice(p * gl, (p + 1) * gl)
            q2 = q_ref[pl.ds(q0, GRID_W), ls] * jnp.asarray(NA_HEAD_DIM ** -0.5, BF16)
            zq = jnp.zeros_like(q2)
            qs = jnp.concatenate(
                [jnp.where((lane >= h * NA_HEAD_DIM) & (lane < (h + 1) * NA_HEAD_DIM), q2, zq)
                 for h in range(HEADS_PER_GROUP)], axis=0)
            kw = k_ref[pl.ds(k0, n_keys), ls]
            vw = v_ref[pl.ds(k0, n_keys), ls]
            sc = _dot_nt(qs, kw)
            sc = jnp.concatenate(
                [sc[:, j * LANES:(j + 1) * LANES] + bias_ref[p, d0 + 2 * j] for j in range(n_keys // LANES)],
                axis=1)
            m = jnp.max(sc, axis=-1, keepdims=True)
            e = jnp.exp(sc - m)
            den = jnp.sum(e, axis=-1, keepdims=True)
            o = _dot(e.astype(BF16), vw) / den
            out = o[0:GRID_W]
            for h in range(1, HEADS_PER_GROUP):
                out = jnp.where(lane >= h * NA_HEAD_DIM, o[h * GRID_W:(h + 1) * GRID_W], out)
            o_ref[pl.ds(q0, GRID_W), ls] = out.astype(BF16)
        return carry

    lax.fori_loop(0, ROWS_PER_STEP, row_body, 0, unroll=8)


def _na_bias_table(rpb):
    qc = np.arange(GRID_W)[:, None]
    kc = np.arange(GRID_W)[None, :]
    dc = np.clip(kc - qc + NA_COLS - 1, 0, 2 * NA_COLS - 2)
    col_start = np.clip(qc - NA_COLS // 2, 0, GRID_W - NA_COLS)
    mask = (kc - col_start >= 0) & (kc - col_start < NA_COLS)
    tab = jnp.where(mask[None, None], rpb[:, :, dc].astype(F32), NEG_INF)
    n_dr = 2 * NA_ROWS - 1
    ng = NA_HEADS // HEADS_PER_GROUP
    tab = tab.reshape(ng, HEADS_PER_GROUP, n_dr, GRID_W, GRID_W).transpose(0, 2, 1, 3, 4)
    tab = tab.reshape(ng, n_dr, HEADS_PER_GROUP * GRID_W, GRID_W)
    return jnp.concatenate([tab[:, :-1], tab[:, 1:]], axis=-1)


def _neighbourhood_attention(u, rpb):
    def kv_spec(col_block):
        def index(s):
            win0, seq_row0 = _na_window_row(s)
            return ((seq_row0 + win0) * GRID_W, col_block * D_NA)
        return pl.BlockSpec((pl.Element(KV_WINDOW_ROWS * GRID_W), pl.Element(D_NA)), index)

    bias = _na_bias_table(rpb)
    return pl.pallas_call(
        _na_kernel,
        grid=(T_ALL // TM_SEQ,),
        in_specs=[pl.BlockSpec((TM_SEQ, D_NA), lambda s: (s, 1)),
                  kv_spec(2), kv_spec(3), _const_spec(bias.shape)],
        out_specs=pl.BlockSpec((TM_SEQ, D_NA), lambda s: (s, 0)),
        out_shape=jax.ShapeDtypeStruct((T_ALL, D_NA), BF16),
        compiler_params=_cparams(), name="neighbourhood_attention")(u, u, u, bias)


def _mix_ln_kernel(ya_ref, yb_ref, xp_ref, xs_ref, w_ref, g_ref, b_ref, o_ref, *, n_prompt_tiles):
    i = pl.program_id(0)
    for rs in _sub_tiles(TM_ROW):
        x = jnp.where(i < n_prompt_tiles, xp_ref[rs, :], xs_ref[rs, :])
        mix = _dot(ya_ref[rs, :], w_ref[:D_POOL, :]) + _dot(yb_ref[rs, :], w_ref[D_POOL:, :])
        o_ref[rs, :] = _layer_norm(DEEPNORM_ALPHA * x + mix, g_ref[...], b_ref[...])


def _mix_ln(ya, yb, xp, xs, w_out, g, b):
    npt = T_PROMPT // TM_ROW
    spec_p, spec_s = _two_group_specs(TM_ROW, npt)
    return pl.pallas_call(
        functools.partial(_mix_ln_kernel, n_prompt_tiles=npt),
        grid=(T_ALL // TM_ROW,),
        in_specs=[pl.BlockSpec((TM_ROW, D_POOL), lambda i: (i, 0)),
                  pl.BlockSpec((TM_ROW, D_NA), lambda i: (i, 0)),
                  spec_p, spec_s,
                  _const_spec((D_MODEL, D_MODEL)), _const_spec((1, D_MODEL)), _const_spec((1, D_MODEL))],
        out_specs=pl.BlockSpec((TM_ROW, D_MODEL), lambda i: (i, 0)),
        out_shape=jax.ShapeDtypeStruct((T_ALL, D_MODEL), F32),
        compiler_params=_cparams(), name="mix_ln1")(ya, yb, xp, xs, w_out, g, b)


def _kv_kernel(m_ref, w_ref, o_ref):
    o_ref[...] = _dot(m_ref[...], w_ref[...]).astype(BF16)


def _memory_kv(mem, w_xkv):
    n_mem_rows = mem.shape[0]
    tn = 1024
    return pl.pallas_call(
        _kv_kernel,
        grid=(2 * D_MODEL // tn,),
        in_specs=[_const_spec((n_mem_rows, D_MODEL)), pl.BlockSpec((D_MODEL, tn), lambda j: (0, j))],
        out_specs=pl.BlockSpec((n_mem_rows, tn), lambda j: (0, j)),
        out_shape=jax.ShapeDtypeStruct((n_mem_rows, 2 * D_MODEL), BF16),
        compiler_params=_cparams(), name="memory_kv")(mem, w_xkv)


def _xattn_kernel(x_ref, wq_ref, k_ref, v_ref, o_ref):
    for rs in _sub_tiles(TM_ROW):
        q = _dot(x_ref[rs, :].astype(BF16), wq_ref[...]).astype(BF16)
        for h in range(XA_HEADS):
            sl = slice(h * XA_HEAD_DIM, (h + 1) * XA_HEAD_DIM)
            sc = _dot_nt(q[:, sl], k_ref[:, sl]) * (XA_HEAD_DIM ** -0.5)
            m = jnp.max(sc, axis=-1, keepdims=True)
            e = jnp.exp(sc - m)
            den = jnp.sum(e, axis=-1, keepdims=True)
            o_ref[rs, sl] = (_dot(e.astype(BF16), v_ref[:, sl]) / den).astype(BF16)


def _mem_batch(i):
    return jnp.minimum((i * TM_ROW) // SEQ, BATCH)


def _cross_attention(x1, w_xq, kv):
    return pl.pallas_call(
        _xattn_kernel,
        grid=(T_ALL // TM_ROW,),
        in_specs=[pl.BlockSpec((TM_ROW, D_MODEL), lambda i: (i, 0)),
                  _const_spec((D_MODEL, D_MODEL)),
                  pl.BlockSpec((N_MEM, D_MODEL), lambda i: (_mem_batch(i), 0)),
                  pl.BlockSpec((N_MEM, D_MODEL), lambda i: (_mem_batch(i), 1))],
        out_specs=pl.BlockSpec((TM_ROW, D_MODEL), lambda i: (i, 0)),
        out_shape=jax.ShapeDtypeStruct((T_ALL, D_MODEL), BF16),
        compiler_params=_cparams(), name="cross_attention")(x1, w_xq, kv, kv)


def _route_kernel(a_ref, x1_ref, wo_ref, g_ref, b_ref, wr_ref, br_ref,
                  x2_ref, x2p_ref, idx_ref, gate_ref, rank_ref, cnt_ref, carry_ref):
    @pl.when(pl.program_id(0) == 0)
    def _():
        carry_ref[...] = jnp.zeros_like(carry_ref)

    projs = [_dot(a_ref[rs, :], wo_ref[...]) for rs in _sub_tiles(TM_ROW)]
    for rs, proj in zip(_sub_tiles(TM_ROW), projs):
        _route_sub_tile(rs, proj, x1_ref, g_ref, b_ref, wr_ref, br_ref,
                        x2_ref, x2p_ref, idx_ref, gate_ref, rank_ref, carry_ref)
    cnt_ref[...] = carry_ref[...]


def _route_sub_tile(rs, proj, x1_ref, g_ref, b_ref, wr_ref, br_ref,
                    x2_ref, x2p_ref, idx_ref, gate_ref, rank_ref, carry_ref):
    n = SUB_ROWS
    x2 = _layer_norm(DEEPNORM_ALPHA * x1_ref[rs, :] + proj, g_ref[...], b_ref[...])
    x2_ref[rs, :] = x2
    half = D_MODEL // 2
    packed = pltpu.pack_elementwise([x2[:, :half], x2[:, half:]], packed_dtype=BF16)
    for s in range(ROW_TILE):
        x2p_ref[pl.ds(rs.start * ROW_TILE + s, n, stride=ROW_TILE), :] = packed[:, s * LANES:(s + 1) * LANES]

    lane = lax.broadcasted_iota(jnp.int32, (n, LANES), 1)
    lanef = lane.astype(F32)
    logits = jnp.where(lane < N_EXPERTS, _dot(x2.astype(BF16), wr_ref[...]) + br_ref[...], -jnp.inf)
    top_v, top_i, hots = [], [], []
    for _ in range(TOP_K):
        m = jnp.max(logits, axis=-1, keepdims=True)
        first = jnp.min(jnp.where(logits == m, lanef, float(LANES)), axis=-1, keepdims=True)
        hot = lanef == first
        top_v.append(m)
        top_i.append(first)
        hots.append(hot)
        logits = jnp.where(hot, -jnp.inf, logits)
    ex = [jnp.exp(v - top_v[0]) for v in top_v]
    den = ex[0] + ex[1] + ex[2] + ex[3]

    chosen = jnp.zeros((n, LANES), F32)
    for hot in hots:
        chosen = jnp.where(hot, 1.0, chosen)
    r_i = lax.broadcasted_iota(jnp.int32, (n, n), 0)
    c_i = lax.broadcasted_iota(jnp.int32, (n, n), 1)
    before = jnp.where(c_i < r_i, 1.0, 0.0).astype(BF16)
    base = carry_ref[...] + _dot(before, chosen.astype(BF16))

    idx_out = jnp.zeros((n, LANES), jnp.int32)
    gate_out = jnp.zeros((n, LANES), F32)
    rank_out = jnp.zeros((n, LANES), jnp.int32)
    for k in range(TOP_K):
        rank_k = jnp.sum(jnp.where(hots[k], base, 0.0), axis=-1, keepdims=True)
        idx_out = jnp.where(lane == k, top_i[k].astype(jnp.int32), idx_out)
        gate_out = jnp.where(lane == k, ex[k] / den, gate_out)
        rank_out = jnp.where(lane == k, rank_k.astype(jnp.int32), rank_out)
    idx_ref[rs, :] = idx_out
    gate_ref[rs, :] = gate_out
    rank_ref[rs, :] = rank_out
    carry_ref[...] = carry_ref[...] + jnp.sum(chosen, axis=0, keepdims=True)


def _project_norm_route(attn, x1, w_xo, g, b, w_router, b_router):
    row = lambda w: pl.BlockSpec((TM_ROW, w), lambda i: (i, 0))
    return pl.pallas_call(
        _route_kernel,
        grid=(T_ALL // TM_ROW,),
        in_specs=[row(D_MODEL), row(D_MODEL), _const_spec((D_MODEL, D_MODEL)),
                  _const_spec((1, D_MODEL)), _const_spec((1, D_MODEL)),
                  _const_spec((D_MODEL, LANES)), _const_spec((1, LANES))],
        out_specs=[row(D_MODEL), pl.BlockSpec((TM_ROW * ROW_TILE, LANES), lambda i: (i, 0)),
                   row(LANES), row(LANES), row(LANES),
                   pl.BlockSpec((1, LANES), lambda i: (0, 0))],
        out_shape=[jax.ShapeDtypeStruct((T_ALL, D_MODEL), F32),
                   jax.ShapeDtypeStruct((T_ALL * ROW_TILE, LANES), jnp.uint32),
                   jax.ShapeDtypeStruct((T_ALL, LANES), jnp.int32),
                   jax.ShapeDtypeStruct((T_ALL, LANES), F32),
                   jax.ShapeDtypeStruct((T_ALL, LANES), jnp.int32),
                   jax.ShapeDtypeStruct((1, LANES), F32)],
        scratch_shapes=[pltpu.VMEM((1, LANES), F32)],
        compiler_params=_cparams(), name="proj_ln2_route")(attn, x1, w_xo, g, b, w_router, b_router)


def _gather_kernel(tok_ref, x2p_ref, o_ref, sem):
    def issue(q, carry):
        for r in range(DMA_GROUP):
            j = q * DMA_GROUP + r
            src = pl.multiple_of(tok_ref[j] * ROW_TILE, ROW_TILE)
            dst = pl.multiple_of(j * ROW_TILE, ROW_TILE)
            pltpu.make_async_copy(x2p_ref.at[pl.ds(src, ROW_TILE)], o_ref.at[pl.ds(dst, ROW_TILE)],
                                  sem).start(priority=r % 2)
        return carry

    lax.fori_loop(0, TS_GATHER // DMA_GROUP, issue, 0)
    pltpu.make_async_copy(x2p_ref.at[pl.ds(0, TS_GATHER * ROW_TILE)], o_ref, sem).wait()


def _gather_rows(slot_tok, x2p):
    return pl.pallas_call(
        _gather_kernel,
        grid=(N_SLOTS // TS_GATHER,),
        in_specs=[pl.BlockSpec((TS_GATHER,), lambda i: (i,), memory_space=pltpu.SMEM),
                  pl.BlockSpec(memory_space=pl.ANY)],
        out_specs=pl.BlockSpec((TS_GATHER * ROW_TILE, LANES), lambda i: (i, 0)),
        out_shape=jax.ShapeDtypeStruct((N_SLOTS * ROW_TILE, LANES), jnp.uint32),
        scratch_shapes=[pltpu.SemaphoreType.DMA],
        compiler_params=_cparams(), name="gather_rows")(slot_tok, x2p)


def _unpack_rows(x_ref, rs):
    lo, hi = [], []
    for s in range(ROW_TILE):
        slab = x_ref[pl.ds(rs.start * ROW_TILE + s, rs.stop - rs.start, stride=ROW_TILE), :]
        lo.append(pltpu.unpack_elementwise(slab, index=0, packed_dtype=BF16, unpacked_dtype=F32).astype(BF16))
        hi.append(pltpu.unpack_elementwise(slab, index=1, packed_dtype=BF16, unpacked_dtype=F32).astype(BF16))
    return jnp.concatenate(lo, axis=1), jnp.concatenate(hi, axis=1)


STEP_VALID = 1
STEP_NEW_TILE = 2


def _gate_up_kernel(blk_ref, exp_ref, wcol_ref, ocol_ref, flag_ref, x_ref, wg_ref, wu_ref, bg_ref, bu_ref,
                    o_ref, wg_bf_ref, wu_bf_ref):
    flags = flag_ref[pl.program_id(0)]

    @pl.when((flags & STEP_NEW_TILE) != 0)
    def _():
        wg_bf_ref[...] = wg_ref[...].astype(BF16)
        wu_bf_ref[...] = wu_ref[...].astype(BF16)

    @pl.when((flags & STEP_VALID) != 0)
    def _():
        half = D_MODEL // 2
        x_lo, x_hi = _unpack_rows(x_ref, slice(0, TM_EXP))
        gate = _dot(x_lo, wg_bf_ref[:half, :]) + _dot(x_hi, wg_bf_ref[half:, :]) + bg_ref[0:1, :]
        up = _dot(x_lo, wu_bf_ref[:half, :]) + _dot(x_hi, wu_bf_ref[half:, :]) + bu_ref[0:1, :]
        gate = jnp.minimum(gate, SWIGLU_LIMIT)
        up = jnp.clip(up, -SWIGLU_LIMIT, SWIGLU_LIMIT)
        o_ref[...] = (gate * jax.nn.sigmoid(SWIGLU_ALPHA * gate) * (up + 1.0)).astype(BF16)

    @pl.when((flags & STEP_VALID) == 0)
    def _():
        o_ref[...] = jnp.zeros_like(o_ref)


def _down_kernel(blk_ref, exp_ref, wcol_ref, ocol_ref, flag_ref, a_ref, w_ref, b_ref, o_ref, w_bf_ref):
    flags = flag_ref[pl.program_id(0)]

    @pl.when((flags & STEP_NEW_TILE) != 0)
    def _():
        w_bf_ref[...] = w_ref[...].astype(BF16)

    @pl.when((flags & STEP_VALID) != 0)
    def _():
        o_ref[...] = _dot(a_ref[...], w_bf_ref[...]) + b_ref[0:1, :]

    @pl.when((flags & STEP_VALID) == 0)
    def _():
        o_ref[...] = jnp.zeros_like(o_ref)


def _pick(table, index):
    hot = index[..., None] == jnp.arange(table.shape[0], dtype=index.dtype)
    return jnp.sum(jnp.where(hot, table, 0), axis=-1)


def _expert_schedule(blocks_per_expert, n_col_tiles):
    n_steps = N_BLOCKS * n_col_tiles
    blk_end = jnp.cumsum(blocks_per_expert)
    blk_start = blk_end - blocks_per_expert
    total = blk_end[-1] * n_col_tiles
    step = jnp.arange(n_steps, dtype=jnp.int32)
    valid = step < total
    s = jnp.minimum(step, total - 1)
    e = jnp.sum((s[:, None] >= (blk_end * n_col_tiles)[None, :]).astype(jnp.int32), axis=1)
    start = _pick(blk_start, e)
    nb = _pick(blocks_per_expert, e)
    local = s - start * n_col_tiles
    col = local // nb
    blk = start + local - col * nb
    tile_id = e * n_col_tiles + col
    new_tile = valid & jnp.concatenate([jnp.ones((1,), bool), tile_id[1:] != tile_id[:-1]])
    flags = valid.astype(jnp.int32) * STEP_VALID + new_tile.astype(jnp.int32) * STEP_NEW_TILE
    spare = step - total
    blk = jnp.where(valid, blk, blk_end[-1] + spare // n_col_tiles)
    out_col = jnp.where(valid, col, spare % n_col_tiles)
    i32 = lambda a: a.astype(jnp.int32)
    return i32(blk), i32(e), i32(col), i32(out_col), flags


def _experts(xs, blocks_per_expert, w_gu, b_gu, w_down, b_down):
    rows = lambda s, blk, ex, wc, oc, fl: (blk[s], 0)
    out = lambda s, blk, ex, wc, oc, fl: (blk[s], oc[s])

    def weight(col_offset):
        return lambda s, blk, ex, wc, oc, fl: (ex[s], 0, col_offset + wc[s])

    n_gu = D_EXPERT // TN_GU
    sched = _expert_schedule(blocks_per_expert, n_gu)
    act = pl.pallas_call(
        _gate_up_kernel,
        grid_spec=pltpu.PrefetchScalarGridSpec(
            num_scalar_prefetch=5, grid=(N_BLOCKS * n_gu,),
            in_specs=[
                pl.BlockSpec((TM_EXP * ROW_TILE, LANES), rows),
                pl.BlockSpec((None, D_MODEL, TN_GU), weight(0)),
                pl.BlockSpec((None, D_MODEL, TN_GU), weight(n_gu)),
                pl.BlockSpec((None, SUBLANES, TN_GU), weight(0)),
                pl.BlockSpec((None, SUBLANES, TN_GU), weight(n_gu)),
            ],
            out_specs=pl.BlockSpec((TM_EXP, TN_GU), out),
            scratch_shapes=[pltpu.VMEM((D_MODEL, TN_GU), BF16), pltpu.VMEM((D_MODEL, TN_GU), BF16)]),
        out_shape=jax.ShapeDtypeStruct((N_SLOTS, D_EXPERT), BF16),
        compiler_params=_cparams(), name="expert_gate_up")(*sched, xs, w_gu, w_gu, b_gu, b_gu)

    n_dn = D_MODEL // TN_DOWN
    sched = _expert_schedule(blocks_per_expert, n_dn)
    return pl.pallas_call(
        _down_kernel,
        grid_spec=pltpu.PrefetchScalarGridSpec(
            num_scalar_prefetch=5, grid=(N_BLOCKS * n_dn,),
            in_specs=[
                pl.BlockSpec((TM_EXP, D_EXPERT), rows),
                pl.BlockSpec((None, D_EXPERT, TN_DOWN), weight(0)),
                pl.BlockSpec((None, SUBLANES, TN_DOWN), weight(0)),
            ],
            out_specs=pl.BlockSpec((TM_EXP, TN_DOWN), out),
            scratch_shapes=[pltpu.VMEM((D_EXPERT, TN_DOWN), BF16)]),
        out_shape=jax.ShapeDtypeStruct((N_SLOTS, D_MODEL), F32),
        compiler_params=_cparams(), name="expert_down")(*sched, act, w_down, b_down)


def _combine_kernel(dest_ref, gate_ref, x2_ref, y_ref, g_ref, b_ref, o_ref, buf_ref, sem):
    def issue(q, carry):
        for r in range(SUBLANES):
            t = q * SUBLANES + r
            for k in range(TOP_K):
                pltpu.make_async_copy(y_ref.at[pl.ds(dest_ref[t * TOP_K + k], 1)],
                                      buf_ref.at[k, q, pl.ds(r, 1)], sem).start(priority=k % 2)
        return carry

    n_tiles = TT_COMB // SUBLANES
    lax.fori_loop(0, n_tiles, issue, 0)
    for k in range(TOP_K):
        for q in range(n_tiles):
            pltpu.make_async_copy(y_ref.at[pl.ds(0, SUBLANES)], buf_ref.at[k, q], sem).wait()
    for rs in _sub_tiles(TT_COMB):
        gates = gate_ref[rs, :]
        tiles = slice(rs.start // SUBLANES, rs.stop // SUBLANES)
        plane = lambda k: buf_ref[k, tiles].reshape(SUB_ROWS, D_MODEL)
        h = gates[:, 0:1] * plane(0)
        for k in range(1, TOP_K):
            h = h + gates[:, k:k + 1] * plane(k)
        o_ref[rs, :] = _layer_norm(DEEPNORM_ALPHA * x2_ref[rs, :] + h, g_ref[...], b_ref[...])


def _combine(dest_flat, gates, x2, y_slots, g, b, first_tile, n_tiles):
    n = TT_COMB * TOP_K
    return pl.pallas_call(
        _combine_kernel,
        grid=(n_tiles,),
        in_specs=[pl.BlockSpec((n,), lambda i: (i + first_tile,), memory_space=pltpu.SMEM),
                  pl.BlockSpec((TT_COMB, LANES), lambda i: (i + first_tile, 0)),
                  pl.BlockSpec((TT_COMB, D_MODEL), lambda i: (i + first_tile, 0)),
                  pl.BlockSpec(memory_space=pl.ANY),
                  _const_spec((1, D_MODEL)), _const_spec((1, D_MODEL))],
        out_specs=pl.BlockSpec((TT_COMB, D_MODEL), lambda i: (i, 0)),
        out_shape=jax.ShapeDtypeStruct((n_tiles * TT_COMB, D_MODEL), F32),
        scratch_shapes=[pltpu.VMEM((TOP_K, TT_COMB // SUBLANES, SUBLANES, D_MODEL), F32),
                        pltpu.SemaphoreType.DMA],
        compiler_params=_cparams(), name="combine_ln3")(dest_flat, gates, x2, y_slots, g, b)


def _slot_tokens(top_i, counts, blocks_per_expert):
    n_assign = T_ALL * TOP_K
    flat = jnp.arange(n_assign, dtype=jnp.int32)
    _, order = lax.sort((top_i.reshape(n_assign), flat), num_keys=1, is_stable=True)
    sorted_tok = order // TOP_K
    blk_end = jnp.cumsum(blocks_per_expert)
    group_start = jnp.cumsum(counts) - counts
    blk = jnp.arange(N_BLOCKS, dtype=jnp.int32)
    e_blk = jnp.minimum(jnp.sum((blk[:, None] >= blk_end[None, :]).astype(jnp.int32), axis=1), N_EXPERTS - 1)
    first_blk = _pick(blk_end - blocks_per_expert, e_blk)
    per_slot = lambda v: jnp.repeat(v, TM_EXP)
    pos = per_slot((blk - first_blk) * TM_EXP) + jnp.tile(jnp.arange(TM_EXP, dtype=jnp.int32), N_BLOCKS)
    valid = per_slot(blk < blk_end[-1]) & (pos < per_slot(_pick(counts, e_blk)))
    src = jnp.clip(per_slot(_pick(group_start, e_blk)) + pos, 0, n_assign - 1)
    spread = jnp.arange(N_SLOTS, dtype=jnp.int32) % T_ALL
    return jnp.where(valid, sorted_tok[src], spread)


def kernel(x_prompt, x_sample, mem_prompt, mem_sample, w_in, w_pool, pool_scale, rpb, w_out,
           ln1_g, ln1_b, w_xq, w_xkv, w_xo, ln2_g, ln2_b,
           w_router, b_router, w_gu, b_gu, w_down, b_down, ln3_g, ln3_b):
    assert w_in.shape[0] == 1, "single-layer problem"
    xp = x_prompt.reshape(T_PROMPT, D_MODEL)
    xs = x_sample.reshape(T_SAMPLE, D_MODEL)
    mem = jnp.concatenate([mem_prompt.reshape(BATCH * N_MEM, D_MODEL),
                           mem_sample.reshape(N_MEM, D_MODEL)], axis=0).astype(BF16)
    row = lambda v: v.reshape(1, -1).astype(F32)

    u = _inproj(xp, xs, w_in[0].astype(BF16))
    ya = _pool_mixer(u, w_pool[0].astype(BF16), row(pool_scale[0]))
    yb = _neighbourhood_attention(u, rpb[0])
    x1 = _mix_ln(ya, yb, xp, xs, w_out[0].astype(BF16), row(ln1_g[0]), row(ln1_b[0]))

    kv = _memory_kv(mem, w_xkv[0].astype(BF16))
    attn = _cross_attention(x1, w_xq[0].astype(BF16), kv)
    w_r = jnp.pad(w_router[0], ((0, 0), (0, LANES - N_EXPERTS))).astype(BF16)
    b_r = jnp.pad(b_router[0].astype(F32), (0, LANES - N_EXPERTS)).reshape(1, LANES)
    x2, x2p, top_i, gates, rank, counts = _project_norm_route(
        attn, x1, w_xo[0].astype(BF16), row(ln2_g[0]), row(ln2_b[0]), w_r, b_r)

    counts = counts[0, :N_EXPERTS].astype(jnp.int32)
    blocks_per_expert = (counts + TM_EXP - 1) // TM_EXP
    blk_end = jnp.cumsum(blocks_per_expert)
    slot_start = (blk_end - blocks_per_expert) * TM_EXP
    dest = (_pick(slot_start, top_i[:, :TOP_K]) + rank[:, :TOP_K]).reshape(T_ALL * TOP_K)
    slot_tok = _slot_tokens(top_i[:, :TOP_K], counts, blocks_per_expert)

    xs_slots = _gather_rows(slot_tok, x2p)
    tiled = lambda bias: jnp.broadcast_to(bias[:, None, :], (N_EXPERTS, SUBLANES, bias.shape[-1]))
    y_slots = _experts(xs_slots, blocks_per_expert, w_gu[0], tiled(b_gu[0]), w_down[0], tiled(b_down[0]))

    g3, b3 = row(ln3_g[0]), row(ln3_b[0])
    n_p = T_PROMPT // TT_COMB
    y_prompt = _combine(dest, gates, x2, y_slots, g3, b3, 0, n_p)
    y_sample = _combine(dest, gates, x2, y_slots, g3, b3, n_p, T_SAMPLE // TT_COMB)
    return (y_prompt.reshape(BATCH, SEQ, D_MODEL), y_sample.reshape(1, DEC_SEQ, D_MODEL))
```
